```python
import jax, jax.numpy as jnp
from jax import lax
import numpy as np

D_MODEL = 1024
BATCH = 2
SEQ = 8192
DEPTH = 1
DEC_BATCH = 128
DEC_SEQ = 8
PAST_LEN = 2048
PAGE_SIZE = 128

POOL_WIDTH = D_MODEL // 2
POOL_WINDOWS = (2, 4, 8, 16)
N_POOL_GROUPS = len(POOL_WINDOWS)
POOL_GROUP = POOL_WIDTH // N_POOL_GROUPS
POOL_HIST = max(POOL_WINDOWS) - 1
N_HEADS = 8
HEAD_DIM = 64
ATTN_WIDTH = N_HEADS * HEAD_DIM
ATTN_SCALE = HEAD_DIM ** -0.5
IDX_HEADS = 8
IDX_DIM = 64
INDEX_SCALE = (IDX_HEADS * IDX_DIM) ** -0.5
TOPK_MAX = 256
Q_BLOCK = 128
N_KEYS = 128
N_EXPERTS = N_KEYS * N_KEYS
PEER_HEADS = 8
PEER_KEY_DIM = 128
PEER_HALF = PEER_KEY_DIM // 2
PEER_TOPK = 16
TOKEN_BLOCK = 512
ALPHA = (2 * DEPTH) ** 0.25
BETA = (8 * DEPTH) ** -0.25
LN_EPS = 1e-5
N_ADA = 6
IN_SPLITS = (POOL_WIDTH, ATTN_WIDTH, ATTN_WIDTH, ATTN_WIDTH, IDX_HEADS * IDX_DIM, IDX_DIM, IDX_HEADS, D_MODEL, D_MODEL)
IN_WIDTH = sum(IN_SPLITS)

kernel_name = 'pool_dsa_peer_hybrid_step'


def alibi_slopes():
    return jnp.exp2(-8.0 * jnp.arange(1, N_HEADS + 1, dtype=jnp.float32) / N_HEADS)


def layer_norm(x, g, b):
    xf = x.astype(jnp.float32)
    mu = jnp.mean(xf, axis=-1, keepdims=True)
    var = jnp.mean(jnp.square(xf - mu), axis=-1, keepdims=True)
    return ((xf - mu) * lax.rsqrt(var + LN_EPS)).astype(x.dtype) * g + b


def split_columns(z):
    offsets = [int(o) for o in np.cumsum(IN_SPLITS)[:-1]]
    return jnp.split(z, offsets, axis=-1)


def multi_scale_pool(u_hist, u, pos0, pool_w, pool_scale):
    b, l, _ = u.shape
    ext = jnp.concatenate([jnp.zeros((b, 1, POOL_WIDTH), jnp.float32), u_hist.astype(jnp.float32), u.astype(jnp.float32)], axis=1)
    cs = jnp.cumsum(ext, axis=1)
    base = 1 + POOL_HIST
    pos = pos0 + jnp.arange(l, dtype=jnp.int32)
    means = []
    for g, w in enumerate(POOL_WINDOWS):
        lo, hi = g * POOL_GROUP, (g + 1) * POOL_GROUP
        wsum = cs[:, base:base + l, lo:hi] - cs[:, base - w:base - w + l, lo:hi]
        cnt = jnp.minimum(pos + 1, w).astype(jnp.float32)[None, :, None]
        means.append(wsum / cnt)
    pooled = (jnp.concatenate(means, axis=-1) - u.astype(jnp.float32)).astype(u.dtype)
    pooled = pooled.reshape(b, l, N_POOL_GROUPS, POOL_GROUP)
    mixed = jnp.einsum('blgc,gcd->blgd', pooled, pool_w).reshape(b, l, POOL_WIDTH)
    return mixed * pool_scale


def indexer_scores(qi, wi, ki):
    rel = jax.nn.relu(jnp.einsum('bthd,bsd->bths', qi, ki).astype(jnp.float32))
    return jnp.einsum('bths,bth->bts', rel, wi.astype(jnp.float32)) * INDEX_SCALE


def gather_rows(x, idx):
    return jax.vmap(lambda xb, ib: xb[ib])(x, idx)


def prompt_sparse_attention(q, k, v, qi, ki, wi):
    b, s = q.shape[:2]
    n_sel = min(TOPK_MAX, s // 4)
    slopes = alibi_slopes()[None, :, None, None]
    key_pos = jnp.arange(s, dtype=jnp.int32)

    def block(i):
        t0 = i * Q_BLOCK
        qb = lax.dynamic_slice_in_dim(q, t0, Q_BLOCK, axis=1)
        qib = lax.dynamic_slice_in_dim(qi, t0, Q_BLOCK, axis=1)
        wib = lax.dynamic_slice_in_dim(wi, t0, Q_BLOCK, axis=1)
        tpos = t0 + jnp.arange(Q_BLOCK, dtype=jnp.int32)
        score = indexer_scores(qib, wib, ki)
        score = jnp.where((key_pos[None, :] <= tpos[:, None])[None], score, -jnp.inf)
        _, idx = lax.top_k(score, n_sel)
        kg = gather_rows(k, idx)
        vg = gather_rows(v, idx)
        logits = jnp.einsum('bthd,btkhd->bhtk', qb, kg).astype(jnp.float32) * ATTN_SCALE
        logits = logits - slopes * jnp.abs(tpos[None, :, None] - idx).astype(jnp.float32)[:, None]
        logits = jnp.where((idx <= tpos[None, :, None])[:, None], logits, -jnp.inf)
        p = jax.nn.softmax(logits, axis=-1).astype(vg.dtype)
        return jnp.einsum('bhtk,btkhd->bthd', p, vg)

    out = lax.map(block, jnp.arange(s // Q_BLOCK, dtype=jnp.int32))
    return jnp.transpose(out, (1, 0, 2, 3, 4)).reshape(b, s, ATTN_WIDTH)


def sample_sparse_attention(q, k_new, v_new, qi, ki_new, wi, cache_k, cache_v, cache_kidx, page_table, layer):
    db, t = q.shape[:2]
    page = cache_k.shape[2]
    past = page_table.shape[1] * page
    n_sel = min(TOPK_MAX, (past + t) // 4)
    tpos = past + jnp.arange(t, dtype=jnp.int32)
    ki_past = cache_kidx[layer, page_table].reshape(db, past, IDX_DIM).astype(ki_new.dtype)
    ki_all = jnp.concatenate([ki_past, ki_new], axis=1)
    key_pos = jnp.arange(past + t, dtype=jnp.int32)
    score = indexer_scores(qi, wi, ki_all)
    score = jnp.where((key_pos[None, :] <= tpos[:, None])[None], score, -jnp.inf)
    _, idx = lax.top_k(score, n_sel)
    in_past = idx < past
    pidx = jnp.minimum(idx, past - 1)
    phys = jax.vmap(lambda pt, i: pt[i])(page_table, pidx // page)
    off = pidx % page
    kg = cache_k[layer, phys, off].astype(q.dtype)
    vg = cache_v[layer, phys, off].astype(v_new.dtype)
    causal_new = tpos[None, :] <= tpos[:, None]
    sel_new = jnp.any(idx[..., None] == tpos[None, None, None, :], axis=2) & causal_new[None]
    slopes = alibi_slopes()[None, :, None, None]
    lp = jnp.einsum('bthd,btkhd->bhtk', q, kg).astype(jnp.float32) * ATTN_SCALE
    lp = lp - slopes * jnp.abs(tpos[None, :, None] - idx).astype(jnp.float32)[:, None]
    lp = jnp.where(in_past[:, None], lp, -jnp.inf)
    lnw = jnp.einsum('bthd,bshd->bhts', q, k_new).astype(jnp.float32) * ATTN_SCALE
    lnw = lnw - slopes * jnp.abs(tpos[:, None] - tpos[None, :]).astype(jnp.float32)[None, None]
    lnw = jnp.where(sel_new[:, None], lnw, -jnp.inf)
    p = jax.nn.softmax(jnp.concatenate([lp, lnw], axis=-1), axis=-1).astype(v_new.dtype)
    o = jnp.einsum('bhtk,btkhd->bthd', p[..., :n_sel], vg) + jnp.einsum('bhts,bshd->bthd', p[..., n_sel:], v_new)
    return o.reshape(db, t, ATTN_WIDTH)


def peer_ffn(h, wq, sub_keys, u_tab, v_tab):
    n = h.shape[0]
    n_blk = -(-n // TOKEN_BLOCK)
    hp = jnp.pad(h, ((0, n_blk * TOKEN_BLOCK - n), (0, 0))).reshape(n_blk, TOKEN_BLOCK, D_MODEL)

    def block(hb):
        q = (hb @ wq).reshape(TOKEN_BLOCK, PEER_HEADS, 2, PEER_HALF)
        s = jnp.einsum('nhcd,ckd->nhck', q, sub_keys).astype(jnp.float32)
        sv, si = lax.top_k(s, PEER_TOPK)
        cand = (sv[:, :, 0, :, None] + sv[:, :, 1, None, :]).reshape(TOKEN_BLOCK, PEER_HEADS, PEER_TOPK * PEER_TOPK)
        cand_e = (si[:, :, 0, :, None] * N_KEYS + si[:, :, 1, None, :]).reshape(TOKEN_BLOCK, PEER_HEADS, PEER_TOPK * PEER_TOPK)
        top_s, top_p = lax.top_k(cand, PEER_TOPK)
        e = jnp.take_along_axis(cand_e, top_p, axis=-1)
        g = jax.nn.softmax(top_s, axis=-1)
        u = u_tab[e]
        act = jax.nn.gelu(jnp.einsum('nhkd,nd->nhk', u, hb).astype(jnp.float32), approximate=False) * g
        return jnp.einsum('nhk,nhkd->nd', act.astype(hb.dtype), v_tab[e])

    return lax.map(block, hp).reshape(n_blk * TOKEN_BLOCK, D_MODEL)[:n]


def run_layer(x, c, token_mix, w_ada, b_ada, w_in, w_branch_a, w_branch_b, w_out, ln1_g, ln1_b,
              peer_wq, peer_sub_keys, peer_u, peer_v, ln2_g, ln2_b):
    b, l, _ = x.shape
    sh1, sc1, g1, sh2, sc2, g2 = jnp.split((c @ w_ada + b_ada)[:, None, :], N_ADA, axis=-1)
    h = x * (1 + sc1) + sh1
    u, q, k, v, qi, ki, wi, ga, gb = split_columns(h @ w_in)
    q = q.reshape(b, l, N_HEADS, HEAD_DIM)
    k = k.reshape(b, l, N_HEADS, HEAD_DIM)
    v = v.reshape(b, l, N_HEADS, HEAD_DIM)
    qi = qi.reshape(b, l, IDX_HEADS, IDX_DIM)
    a, o, state = token_mix(u, q, k, v, qi, ki, wi)
    merged = jax.nn.sigmoid(ga) * (a @ w_branch_a) + jax.nn.sigmoid(gb) * (o @ w_branch_b)
    x = layer_norm(ALPHA * x + g1 * (merged @ w_out), ln1_g, ln1_b)
    h2 = x * (1 + sc2) + sh2
    f = peer_ffn(h2.reshape(b * l, D_MODEL), peer_wq, peer_sub_keys, peer_u, peer_v).reshape(b, l, D_MODEL)
    x = layer_norm(ALPHA * x + g2 * f, ln2_g, ln2_b)
    return x, state


def setup_inputs(seed: int = 0) -> dict:
    key = jax.random.key(seed)
    ks = jax.random.split(key, 32)
    n_pages = PAST_LEN // PAGE_SIZE
    n_used = DEC_BATCH * n_pages
    n_phys = n_used + max(1, n_used // 4)
    d = D_MODEL

    def nrm(k, shape, scale):
        return jax.random.normal(k, shape, jnp.float32) * scale

    page_table = jax.random.permutation(ks[0], n_phys)[:n_used].reshape(DEC_BATCH, n_pages).astype(jnp.int32)
    return {
        'x_prompt': nrm(ks[1], (BATCH, SEQ, d), 1.0),
        'x_sample': nrm(ks[2], (DEC_BATCH, DEC_SEQ, d), 1.0),
        'cache_k': nrm(ks[3], (DEPTH, n_phys, PAGE_SIZE, N_HEADS, HEAD_DIM), 1.0),
        'cache_v': nrm(ks[4], (DEPTH, n_phys, PAGE_SIZE, N_HEADS, HEAD_DIM), 1.0),
        'cache_kidx': nrm(ks[5], (DEPTH, n_phys, PAGE_SIZE, IDX_DIM), 1.0),
        'state_pool': nrm(ks[6], (DEPTH, DEC_BATCH, POOL_HIST, POOL_WIDTH), 1.0),
        'page_table': page_table,
        'c_prompt': nrm(ks[7], (BATCH, d), 1.0),
        'c_sample': nrm(ks[8], (DEC_BATCH, d), 1.0),
        'w_ada': nrm(ks[9], (DEPTH, d, N_ADA * d), 0.3 * d ** -0.5),
        'b_ada': nrm(ks[10], (DEPTH, N_ADA * d), 0.01),
        'w_in': nrm(ks[11], (DEPTH, d, IN_WIDTH), d ** -0.5),
        'pool_w': nrm(ks[12], (DEPTH, N_POOL_GROUPS, POOL_GROUP, POOL_GROUP), POOL_GROUP ** -0.5),
        'pool_scale': 1.0 + nrm(ks[13], (DEPTH, POOL_WIDTH), 0.1),
        'w_branch_a': nrm(ks[14], (DEPTH, POOL_WIDTH, d), BETA * POOL_WIDTH ** -0.5),
        'w_branch_b': nrm(ks[15], (DEPTH, ATTN_WIDTH, d), BETA * ATTN_WIDTH ** -0.5),
        'w_out': nrm(ks[16], (DEPTH, d, d), BETA * d ** -0.5),
        'ln1_g': 1.0 + nrm(ks[17], (DEPTH, d), 0.05),
        'ln1_b': nrm(ks[18], (DEPTH, d), 0.01),
        'peer_wq': nrm(ks[19], (DEPTH, d, PEER_HEADS * PEER_KEY_DIM), d ** -0.5),
        'peer_sub_keys': nrm(ks[20], (DEPTH, 2, N_KEYS, PEER_HALF), PEER_HALF ** -0.5),
        'peer_u': nrm(ks[21], (DEPTH, N_EXPERTS, d), d ** -0.5),
        'peer_v': nrm(ks[22], (DEPTH, N_EXPERTS, d), BETA),
        'ln2_g': 1.0 + nrm(ks[23], (DEPTH, d), 0.05),
        'ln2_b': nrm(ks[24], (DEPTH, d), 0.01),
    }


def reference(x_prompt, x_sample, cache_k, cache_v, cache_kidx, state_pool, page_table, c_prompt, c_sample,
              w_ada, b_ada, w_in, pool_w, pool_scale, w_branch_a, w_branch_b, w_out, ln1_g, ln1_b,
              peer_wq, peer_sub_keys, peer_u, peer_v, ln2_g, ln2_b):
    past = page_table.shape[1] * cache_k.shape[2]
    y_prompt, y_sample = x_prompt, x_sample
    kp, vp, kip, pp, ksm, vsm, kism, psm = [], [], [], [], [], [], [], []
    for l in range(DEPTH):
        shared = (w_ada[l], b_ada[l], w_in[l], w_branch_a[l], w_branch_b[l], w_out[l], ln1_g[l], ln1_b[l],
                  peer_wq[l], peer_sub_keys[l], peer_u[l], peer_v[l], ln2_g[l], ln2_b[l])
        pw, psc = pool_w[l], pool_scale[l]

        def prompt_mix(u, q, k, v, qi, ki, wi, pw=pw, psc=psc):
            hist = jnp.zeros((u.shape[0], POOL_HIST, POOL_WIDTH), u.dtype)
            a = multi_scale_pool(hist, u, 0, pw, psc)
            o = prompt_sparse_attention(q, k, v, qi, ki, wi)
            tail = jnp.concatenate([hist, u], axis=1)[:, -POOL_HIST:]
            return a, o, (k, v, ki, tail)

        def sample_mix(u, q, k, v, qi, ki, wi, pw=pw, psc=psc, l=l):
            hist = state_pool[l].astype(u.dtype)
            a = multi_scale_pool(hist, u, past, pw, psc)
            o = sample_sparse_attention(q, k, v, qi, ki, wi, cache_k, cache_v, cache_kidx, page_table, l)
            tail = jnp.concatenate([hist, u], axis=1)[:, -POOL_HIST:]
            return a, o, (k, v, ki, tail)

        y_prompt, st_p = run_layer(y_prompt, c_prompt, prompt_mix, *shared)
        y_sample, st_s = run_layer(y_sample, c_sample, sample_mix, *shared)
        kp.append(st_p[0]); vp.append(st_p[1]); kip.append(st_p[2]); pp.append(st_p[3])
        ksm.append(st_s[0]); vsm.append(st_s[1]); kism.append(st_s[2]); psm.append(st_s[3])
    return (y_prompt, y_sample, jnp.stack(kp), jnp.stack(vp), jnp.stack(kip), jnp.stack(pp),
            jnp.stack(ksm), jnp.stack(vsm), jnp.stack(kism), jnp.stack(psm))
```

```python
import functools
import math

import jax
import jax.numpy as jnp
import numpy as np
from jax import lax
from jax.experimental import pallas as pl
from jax.experimental.pallas import tpu as pltpu

F32, BF16, I32 = jnp.float32, jnp.bfloat16, jnp.int32

POOL_WINDOWS = (2, 4, 8, 16)
IDX_HEADS = 8
TOPK_MAX = 256
PEER_HEADS = 8
PEER_TOPK = 16
LN_EPS = 1e-5
N_ADA = 6

VMEM_LIMIT_BYTES = 56 * 1024 * 1024
SUBLANES = 8
LANES = 128
TOKEN_TILE = 256
PEER_TOKEN_TILE = 512
PEER_EXPERT_TILE = 1024
HIST_ROWS = 16

INT_MIN = -(2 ** 31)
KEY_NEG_INF = -2139095041
NEG_BIG = -1e30


def _cparams(sem):
    return pltpu.CompilerParams(dimension_semantics=sem, vmem_limit_bytes=VMEM_LIMIT_BYTES)


def _dot(a, b):
    return jnp.dot(a, b, preferred_element_type=F32)


def _dot_nt(a, b):
    return lax.dot_general(a, b, (((1,), (1,)), ((), ())), preferred_element_type=F32)


def _key_to_float(key):
    return lax.bitcast_convert_type(key ^ ((key >> 31) & 0x7FFFFFFF), F32)


def _threshold_step(i, key, count_ge, n_sel):
    trial = key + lax.shift_left(jnp.int32(1), 31 - i)
    ok = jnp.logical_or(count_ge(_key_to_float(trial)) >= n_sel, trial < KEY_NEG_INF)
    return jnp.where(ok, trial, key)


def _ada_kernel(c_ref, w_ref, b_ref, o_ref):
    o_ref[...] = _dot(c_ref[...].astype(BF16), w_ref[...].astype(BF16)) + b_ref[...]


def _ada(c_all, w_ada, b_ada):
    n, d = c_all.shape
    return pl.pallas_call(
        _ada_kernel,
        grid=(N_ADA,),
        in_specs=[pl.BlockSpec((n, d), lambda j: (0, 0)),
                  pl.BlockSpec((d, d), lambda j: (0, j)),
                  pl.BlockSpec((1, d), lambda j: (0, j))],
        out_specs=pl.BlockSpec((None, n, d), lambda j: (j, 0, 0)),
        out_shape=jax.ShapeDtypeStruct((N_ADA, n, d), F32),
        compiler_params=_cparams(("arbitrary",)),
        name="ada",
    )(c_all, w_ada, b_ada.reshape(1, -1))


def _grouped(x, mod, rows):
    nseq, l, d = x.shape
    if l >= rows:
        assert l % rows == 0
        per = l // rows
        x3 = x.reshape(nseq * per, rows, d)
        mod4 = jnp.broadcast_to(mod[:, :, None, None, :], (N_ADA, nseq, per, 1, d)).reshape(N_ADA, nseq * per, 1, d)
        return x3, mod4, 1
    assert rows % l == 0 and nseq % (rows // l) == 0
    return x, mod[:, :, None, :], rows // l


def _proj_kernel(x_ref, mod_ref, wn_ref, wki_ref, wg_ref, wt_ref,
                 u_ref, k_ref, v_ref, ki_ref, kb_ref, kib_ref, vt_ref, qt_ref, qit_ref, wit_ref, ga_ref, gb_ref,
                 *, pw, aw, iw, q_scale):
    ks, r, d = x_ref.shape
    h = (x_ref[...] * (1.0 + mod_ref[1]) + mod_ref[0]).reshape(ks * r, d).astype(BF16)
    u_ref[...] = _dot(h, wn_ref[:, 0:pw])
    k = _dot(h, wn_ref[:, pw:pw + aw])
    k_ref[...] = k
    kb_ref[...] = k.astype(BF16)
    v_ref[...] = _dot(h, wn_ref[:, pw + aw:pw + 2 * aw])
    ki = _dot(h, wki_ref[...])
    ki_ref[...] = ki
    kib_ref[...] = ki.astype(BF16)
    ga_ref[...] = _dot(h, wg_ref[:, 0:d])
    gb_ref[...] = _dot(h, wg_ref[:, d:2 * d])
    vt_ref[...] = _dot_nt(wt_ref[0:aw, :], h).astype(BF16)
    qt_ref[...] = (_dot_nt(wt_ref[aw:2 * aw, :], h) * q_scale).astype(BF16)
    qit_ref[...] = _dot_nt(wt_ref[2 * aw:2 * aw + iw, :], h).astype(BF16)
    wit_ref[...] = _dot_nt(wt_ref[2 * aw + iw:2 * aw + iw + 2 * SUBLANES, :], h)


def _proj(x3, mod4, ks, wn, wki, wg, wt, *, pw, aw, iw, idim, q_scale):
    g, r, d = x3.shape
    tm = ks * r
    nb = g // ks
    n = g * r
    row = lambda cols: pl.BlockSpec((tm, cols), lambda i: (i, 0))
    col = lambda rows: pl.BlockSpec((None, rows, tm), lambda i: (i, 0, 0))
    full = lambda a: pl.BlockSpec(a.shape, lambda i: (0,) * a.ndim)
    outs = [((n, pw), F32, row(pw)), ((n, aw), F32, row(aw)), ((n, aw), F32, row(aw)), ((n, idim), F32, row(idim)),
            ((n, aw), BF16, row(aw)), ((n, idim), BF16, row(idim)),
            ((nb, aw, tm), BF16, col(aw)), ((nb, aw, tm), BF16, col(aw)), ((nb, iw, tm), BF16, col(iw)),
            ((nb, 2 * SUBLANES, tm), F32, col(2 * SUBLANES)),
            ((n, d), F32, row(d)), ((n, d), F32, row(d))]
    return pl.pallas_call(
        functools.partial(_proj_kernel, pw=pw, aw=aw, iw=iw, q_scale=q_scale),
        grid=(nb,),
        in_specs=[pl.BlockSpec((ks, r, d), lambda i: (i, 0, 0)),
                  pl.BlockSpec((N_ADA, ks, 1, d), lambda i: (0, i, 0, 0)),
                  full(wn), full(wki), full(wg), full(wt)],
        out_specs=[o[2] for o in outs],
        out_shape=[jax.ShapeDtypeStruct(o[0], o[1]) for o in outs],
        compiler_params=_cparams(("arbitrary",)),
        name="in_proj",
    )(x3, mod4, wn, wki, wg, wt)


def _pool_windows(ext_ref, pos, pw_ref, ps_ref, a_ref):
    ks, rows, width = ext_ref.shape
    r = rows - HIST_ROWS
    gw = width // len(POOL_WINDOWS)
    for g, w in enumerate(POOL_WINDOWS):
        lo = g * gw
        cur = ext_ref[:, HIST_ROWS:HIST_ROWS + r, lo:lo + gw]
        acc = cur
        for j in range(1, w):
            acc = acc + ext_ref[:, HIST_ROWS - j:HIST_ROWS - j + r, lo:lo + gw]
        cnt = jnp.minimum(pos + 1, w).astype(F32)
        pooled = (acc / cnt - cur).reshape(ks * r, gw).astype(BF16)
        mixed = _dot(pooled, pw_ref[g])
        a_ref[:, lo:lo + gw] = (mixed * ps_ref[:, lo:lo + gw]).astype(BF16)


def _pool_prompt_kernel(u_ref, pw_ref, ps_ref, a_ref, ext_ref):
    i = pl.program_id(1)
    tp, width = u_ref.shape
    gw = width // len(POOL_WINDOWS)

    @pl.when(i == 0)
    def _():
        ext_ref[:, 0:HIST_ROWS, :] = jnp.zeros((1, HIST_ROWS, width), F32)

    @pl.when(i > 0)
    def _():
        ext_ref[:, 0:HIST_ROWS, :] = ext_ref[:, tp:tp + HIST_ROWS, :]

    ext_ref[:, HIST_ROWS:, :] = u_ref[...][None]
    pos = i * tp + lax.broadcasted_iota(I32, (1, tp, gw), 1)
    _pool_windows(ext_ref, pos, pw_ref, ps_ref, a_ref)


def _pool_prompt(u, pool_w, pool_scale):
    b, s, width = u.shape
    tp = TOKEN_TILE
    return pl.pallas_call(
        _pool_prompt_kernel,
        grid=(b, s // tp),
        in_specs=[pl.BlockSpec((None, tp, width), lambda bi, i: (bi, i, 0)),
                  pl.BlockSpec(pool_w.shape, lambda bi, i: (0, 0, 0)),
                  pl.BlockSpec(pool_scale.shape, lambda bi, i: (0, 0))],
        out_specs=pl.BlockSpec((None, tp, width), lambda bi, i: (bi, i, 0)),
        out_shape=jax.ShapeDtypeStruct((b, s, width), BF16),
        scratch_shapes=[pltpu.VMEM((1, HIST_ROWS + tp, width), F32)],
        compiler_params=_cparams(("arbitrary", "arbitrary")),
        name="pool_prompt",
    )(u, pool_w, pool_scale)


def _pool_sample_kernel(u_ref, hist_ref, pw_ref, ps_ref, a_ref, ext_ref, *, pos0):
    ks, t, width = u_ref.shape
    gw = width // len(POOL_WINDOWS)
    ext_ref[:, 0:HIST_ROWS, :] = hist_ref[...]
    ext_ref[:, HIST_ROWS:, :] = u_ref[...]
    pos = pos0 + lax.broadcasted_iota(I32, (1, t, gw), 1)
    _pool_windows(ext_ref, pos, pw_ref, ps_ref, a_ref)


def _pool_sample(u, hist16, pool_w, pool_scale, pos0):
    db, t, width = u.shape
    ks = TOKEN_TILE // t
    return pl.pallas_call(
        functools.partial(_pool_sample_kernel, pos0=pos0),
        grid=(db // ks,),
        in_specs=[pl.BlockSpec((ks, t, width), lambda i: (i, 0, 0)),
                  pl.BlockSpec((ks, HIST_ROWS, width), lambda i: (i, 0, 0)),
                  pl.BlockSpec(pool_w.shape, lambda i: (0, 0, 0)),
                  pl.BlockSpec(pool_scale.shape, lambda i: (0, 0))],
        out_specs=pl.BlockSpec((ks * t, width), lambda i: (i, 0)),
        out_shape=jax.ShapeDtypeStruct((db * t, width), BF16),
        scratch_shapes=[pltpu.VMEM((ks, HIST_ROWS + t, width), F32)],
        compiler_params=_cparams(("arbitrary",)),
        name="pool_sample",
    )(u, hist16, pool_w, pool_scale)


def _alibi_slopes(n_heads):
    return [2.0 ** (-8.0 * (h + 1) / n_heads) for h in range(n_heads)]


def _attn_prompt_kernel(qt_ref, qit_ref, wit_ref, kb_ref, vt_ref, kib_ref, o_ref,
                        sc_sc, qz_sc, m_sc, l_sc, acc_sc, *, n_sel, idx_scale, n_heads, hd, idim, seq_len):
    qb = pl.program_id(1)
    tq = qt_ref.shape[-1]
    sk = tq
    nchunk = qb + 1
    t_idx = qb * tq + lax.broadcasted_iota(I32, (sk, tq), 1)
    s_loc = lax.broadcasted_iota(I32, (sk, tq), 0)

    def score_body(c, carry):
        kic = kib_ref[c]
        acc = jnp.zeros((sk, tq), F32)
        for h in range(IDX_HEADS):
            rel = _dot(kic, qit_ref[h * idim:(h + 1) * idim, :])
            acc = acc + wit_ref[h:h + 1, :] * jnp.maximum(rel, 0.0)
        sc_sc[c] = jnp.where(c * sk + s_loc <= t_idx, acc * idx_scale, -jnp.inf)
        return carry

    lax.fori_loop(0, nchunk, score_body, 0)

    def colsum(mask):
        return jnp.where(mask, 1.0, 0.0).reshape(sk // SUBLANES, SUBLANES, tq).sum(axis=0)

    def count(pred):
        acc = lax.fori_loop(0, nchunk, lambda c, a: a + colsum(pred(c, sc_sc[c])), jnp.zeros((SUBLANES, tq), F32))
        return acc.sum(axis=0, keepdims=True)

    count_ge = lambda t: count(lambda c, s: s >= t)
    tau_key = lax.fori_loop(0, 32, lambda i, k: _threshold_step(i, k, count_ge, n_sel),
                            jnp.full((1, tq), INT_MIN, I32))
    tau = _key_to_float(tau_key)

    tie_rows = jnp.logical_and(count_ge(tau) > n_sel, tau_key > KEY_NEG_INF)
    has_tie = jnp.max(jnp.where(tie_rows, 1.0, 0.0)) > 0.5

    @pl.when(has_tie)
    def _():
        need = n_sel - count(lambda c, s: s > tau)
        nbits = max(1, int(math.ceil(math.log2(seq_len))))

        def first_enough(i, j):
            step = lax.shift_left(jnp.int32(1), nbits - 1 - i)
            upto = count(lambda c, s: jnp.logical_and(s == tau, c * sk + s_loc <= j + step - 1))
            return jnp.where(upto < need, j + step, j)

        last = lax.fori_loop(0, nbits, first_enough, jnp.zeros((1, tq), I32))

        def demote(c, carry):
            s = sc_sc[c]
            drop = jnp.logical_and(jnp.logical_and(tie_rows, s == tau), c * sk + s_loc > last)
            sc_sc[c] = jnp.where(drop, -jnp.inf, s)
            return carry

        lax.fori_loop(0, nchunk, demote, 0)

    pair_rows = lax.broadcasted_iota(I32, (2 * hd, tq), 0)
    for h in range(n_heads):
        blk = qt_ref[(h // 2) * 2 * hd:(h // 2 + 1) * 2 * hd, :]
        keep = pair_rows < hd if h % 2 == 0 else pair_rows >= hd
        qz_sc[h] = jnp.where(keep, blk, jnp.zeros_like(blk))
    m_sc[...] = jnp.full(m_sc.shape, NEG_BIG, F32)
    l_sc[...] = jnp.zeros(l_sc.shape, F32)
    acc_sc[...] = jnp.zeros(acc_sc.shape, F32)
    slopes = _alibi_slopes(n_heads)

    def att_body(c, carry):
        pos = c * sk + s_loc
        sel = jnp.logical_and(sc_sc[c] >= tau, pos <= t_idx)
        dist = (pos - qb * tq).astype(F32)
        for h in range(n_heads):
            lg = _dot(kb_ref[c, :, (h // 2) * 2 * hd:(h // 2 + 1) * 2 * hd], qz_sc[h])
            lg = jnp.where(sel, lg + slopes[h] * dist, -jnp.inf)
            m_old = m_sc[h:h + 1, :]
            m_new = jnp.maximum(m_old, jnp.max(lg, axis=0, keepdims=True))
            alpha = jnp.exp(m_old - m_new)
            p = jnp.exp(lg - m_new)
            l_sc[h:h + 1, :] = alpha * l_sc[h:h + 1, :] + jnp.sum(p, axis=0, keepdims=True)
            pv = _dot(vt_ref[c, h * hd:(h + 1) * hd, :], p.astype(BF16))
            acc_sc[h * hd:(h + 1) * hd, :] = alpha * acc_sc[h * hd:(h + 1) * hd, :] + pv
            m_sc[h:h + 1, :] = m_new
        return carry

    lax.fori_loop(0, nchunk, att_body, 0)
    out_t = jnp.concatenate([acc_sc[h * hd:(h + 1) * hd, :] / l_sc[h:h + 1, :] for h in range(n_heads)], axis=0)
    o_ref[...] = out_t.T.astype(BF16)


def _attn_prompt(qt, qit, wit, kb, vt, kib, *, n_sel, idx_scale, n_heads, hd, idim):
    b, nck, aw, tq = qt.shape
    iw = qit.shape[2]
    resident = lambda shape: pl.BlockSpec((None,) + shape, lambda bi, i: (bi, 0, 0, 0), pipeline_mode=pl.Buffered(1))
    tile = lambda rows: pl.BlockSpec((None, None, rows, tq), lambda bi, i: (bi, i, 0, 0))
    return pl.pallas_call(
        functools.partial(_attn_prompt_kernel, n_sel=n_sel, idx_scale=idx_scale, n_heads=n_heads, hd=hd, idim=idim,
                          seq_len=nck * tq),
        grid=(b, nck),
        in_specs=[tile(aw), tile(iw), tile(2 * SUBLANES),
                  resident((nck, tq, aw)), resident((nck, aw, tq)), resident((nck, tq, idim))],
        out_specs=pl.BlockSpec((None, tq, aw), lambda bi, i: (bi, i, 0)),
        out_shape=jax.ShapeDtypeStruct((b, nck * tq, aw), BF16),
        scratch_shapes=[pltpu.VMEM((nck, tq, tq), F32),
                        pltpu.VMEM((n_heads, 2 * hd, tq), BF16),
                        pltpu.VMEM((n_heads, tq), F32),
                        pltpu.VMEM((n_heads, tq), F32),
                        pltpu.VMEM((aw, tq), F32)],
        compiler_params=_cparams(("arbitrary", "arbitrary")),
        name="attn_prompt",
    )(qt, qit, wit, kb, vt, kib)


def _attn_sample_kernel(pt_ref, q_ref, qi_ref, wi_ref, knew_ref, vnew_ref, kinew_ref, expand_ref, *rest,
                        n_pages, page, n_sel, idx_scale, n_heads, hd, t_new):
    del pt_ref
    kidx_refs, k_refs, v_refs = rest[0:n_pages], rest[n_pages:2 * n_pages], rest[2 * n_pages:3 * n_pages]
    o_ref, sc_sc, lg_sc = rest[3 * n_pages:]
    past = n_pages * page
    width = past + page
    lanes_exp = page * n_heads
    row = lax.broadcasted_iota(I32, (t_new, page), 0)
    lane = lax.broadcasted_iota(I32, (t_new, page), 1)

    def scores(ki):
        acc = jnp.zeros((t_new, page), F32)
        for h in range(IDX_HEADS):
            acc = acc + wi_ref[h] * jnp.maximum(_dot_nt(qi_ref[h], ki), 0.0)
        return acc * idx_scale

    for p in range(n_pages):
        sc_sc[:, p * page:(p + 1) * page] = scores(kidx_refs[p][...].astype(BF16))
    sc_sc[:, past:width] = jnp.where(lane <= row, scores(kinew_ref[...]), -jnp.inf)

    def count(pred):
        return jnp.sum(jnp.where(pred(sc_sc[...]), 1.0, 0.0), axis=1, keepdims=True)

    count_ge = lambda t: count(lambda s: s >= t)
    tau_key = lax.fori_loop(0, 32, lambda i, k: _threshold_step(i, k, count_ge, n_sel),
                            jnp.full((t_new, 1), INT_MIN, I32))
    tau = _key_to_float(tau_key)

    tie_rows = jnp.logical_and(count_ge(tau) > n_sel, tau_key > KEY_NEG_INF)
    has_tie = jnp.max(jnp.where(tie_rows, 1.0, 0.0)) > 0.5

    @pl.when(has_tie)
    def _():
        need = n_sel - count(lambda s: s > tau)
        pos = lax.broadcasted_iota(I32, (t_new, width), 1)
        nbits = max(1, int(math.ceil(math.log2(width))))

        def first_enough(i, j):
            step = lax.shift_left(jnp.int32(1), nbits - 1 - i)
            upto = count(lambda s: jnp.logical_and(s == tau, pos <= j + step - 1))
            return jnp.where(upto < need, j + step, j)

        last = lax.fori_loop(0, nbits, first_enough, jnp.zeros((t_new, 1), I32))
        s = sc_sc[...]
        drop = jnp.logical_and(jnp.logical_and(tie_rows, s == tau), pos > last)
        sc_sc[...] = jnp.where(drop, -jnp.inf, s)

    slopes = _alibi_slopes(n_heads)
    lane_e = lax.broadcasted_iota(I32, (t_new, lanes_exp), 1)
    row_e = lax.broadcasted_iota(I32, (t_new, lanes_exp), 0)
    tok_e = lane_e // n_heads
    head_e = lane_e - tok_e * n_heads
    expand = expand_ref[...]

    def page_logits(p, k2, lanes, causal):
        sel01 = jnp.where(sc_sc[:, p * page:(p + 1) * page] >= tau, 1.0, 0.0).astype(BF16)
        sel = _dot(sel01, expand[:, 0:lanes]) > 0.5
        if causal:
            sel = jnp.logical_and(sel, tok_e[:, 0:lanes] <= row_e[:, 0:lanes])
        dist = ((past + row_e[:, 0:lanes]) - (p * page + tok_e[:, 0:lanes])).astype(F32)
        for h in range(n_heads):
            lg = _dot_nt(q_ref[h], k2) - slopes[h] * dist
            ok = jnp.logical_and(sel, head_e[:, 0:lanes] == h)
            lg_sc[h, :, p * lanes_exp:p * lanes_exp + lanes] = jnp.where(ok, lg, -jnp.inf)

    for p in range(n_pages):
        page_logits(p, k_refs[p][...].reshape(page * n_heads, hd).astype(BF16), lanes_exp, False)
    new_lanes = knew_ref.shape[0]
    page_logits(n_pages, knew_ref[...], new_lanes, True)

    denoms, accs = [], []
    for h in range(n_heads):
        lg = lg_sc[h]
        pr = jnp.exp(lg - jnp.max(lg, axis=1, keepdims=True))
        denoms.append(jnp.sum(pr, axis=1, keepdims=True))
        lg_sc[h] = pr
        accs.append(_dot(pr[:, past * n_heads:past * n_heads + new_lanes].astype(BF16), vnew_ref[...]))
    for p in range(n_pages):
        v2 = v_refs[p][...].reshape(page * n_heads, hd).astype(BF16)
        for h in range(n_heads):
            accs[h] = accs[h] + _dot(lg_sc[h, :, p * lanes_exp:(p + 1) * lanes_exp].astype(BF16), v2)
    for h in range(n_heads):
        o_ref[h] = accs[h] / denoms[h]


def _attn_sample(page_table, q_s, qi_s, wi_s, knew, vnew, kinew, cache_k, cache_v, cache_kidx, layer,
                 *, n_sel, idx_scale):
    db, n_heads, t_new, hd = q_s.shape
    n_pages = page_table.shape[1]
    page = cache_k.shape[2]
    idim = cache_kidx.shape[3]
    new_lanes = knew.shape[1]
    expand = (np.arange(page * n_heads)[None, :] // n_heads == np.arange(page)[:, None]).astype(np.float32)
    expand = jnp.asarray(expand, BF16)
    per_seq = lambda a: pl.BlockSpec((None,) + a.shape[1:], lambda b, pt: (b,) + (0,) * (a.ndim - 1))
    kv_page = lambda j: pl.BlockSpec((None, None, page, n_heads, hd), lambda b, pt: (layer, pt[b, j], 0, 0, 0))
    ki_page = lambda j: pl.BlockSpec((None, None, page, idim), lambda b, pt: (layer, pt[b, j], 0, 0))
    in_specs = ([per_seq(q_s), per_seq(qi_s), per_seq(wi_s), per_seq(knew), per_seq(vnew), per_seq(kinew),
                 pl.BlockSpec(expand.shape, lambda b, pt: (0, 0))]
                + [ki_page(j) for j in range(n_pages)]
                + [kv_page(j) for j in range(n_pages)]
                + [kv_page(j) for j in range(n_pages)])
    grid_spec = pltpu.PrefetchScalarGridSpec(
        num_scalar_prefetch=1,
        grid=(db,),
        in_specs=in_specs,
        out_specs=pl.BlockSpec((None, n_heads, t_new, hd), lambda b, pt: (b, 0, 0, 0)),
        scratch_shapes=[pltpu.VMEM((t_new, (n_pages + 1) * page), F32),
                        pltpu.VMEM((n_heads, t_new, n_pages * page * n_heads + new_lanes), F32)],
    )
    return pl.pallas_call(
        functools.partial(_attn_sample_kernel, n_pages=n_pages, page=page, n_sel=n_sel, idx_scale=idx_scale,
                          n_heads=n_heads, hd=hd, t_new=t_new),
        grid_spec=grid_spec,
        out_shape=jax.ShapeDtypeStruct((db, n_heads, t_new, hd), F32),
        compiler_params=_cparams(("arbitrary",)),
        name="attn_sample",
    )(page_table, q_s, qi_s, wi_s, knew, vnew, kinew, expand,
      *([cache_kidx] * n_pages), *([cache_k] * n_pages), *([cache_v] * n_pages))


def _layer_norm(y, g, b):
    mu = jnp.mean(y, axis=-1, keepdims=True)
    var = jnp.mean(jnp.square(y - mu), axis=-1, keepdims=True)
    return (y - mu) * lax.rsqrt(var + LN_EPS) * g + b


def _merge_kernel(x_ref, mod_ref, a_ref, o_ref, ga_ref, gb_ref, wa_ref, wb_ref, wo_ref, g_ref, b_ref, x1_ref, *, alpha):
    ks, r, d = x_ref.shape
    merged = (jax.nn.sigmoid(ga_ref[...]) * _dot(a_ref[...], wa_ref[...])
              + jax.nn.sigmoid(gb_ref[...]) * _dot(o_ref[...], wb_ref[...]))
    upd = _dot(merged.astype(BF16), wo_ref[...]).reshape(ks, r, d)
    y = alpha * x_ref[...] + mod_ref[2] * upd
    x1_ref[...] = _layer_norm(y, g_ref[...], b_ref[...])


def _merge(x3, mod4, ks, a, o, ga, gb, wa, wb, wo, ln_g, ln_b, alpha):
    g, r, d = x3.shape
    tm = ks * r
    row = lambda arr: pl.BlockSpec((tm, arr.shape[1]), lambda i: (i, 0))
    full = lambda arr: pl.BlockSpec(arr.shape, lambda i: (0,) * arr.ndim)
    return pl.pallas_call(
        functools.partial(_merge_kernel, alpha=alpha),
        grid=(g // ks,),
        in_specs=[pl.BlockSpec((ks, r, d), lambda i: (i, 0, 0)),
                  pl.BlockSpec((N_ADA, ks, 1, d), lambda i: (0, i, 0, 0)),
                  row(a), row(o), row(ga), row(gb), full(wa), full(wb), full(wo), full(ln_g), full(ln_b)],
        out_specs=pl.BlockSpec((ks, r, d), lambda i: (i, 0, 0)),
        out_shape=jax.ShapeDtypeStruct((g, r, d), F32),
        compiler_params=_cparams(("arbitrary",)),
        name="merge_ln1",
    )(x3, mod4, a, o, ga, gb, wa, wb, wo, ln_g, ln_b)


def _top_rows(s, n_top):
    rows = s.shape[0]
    row_id = lax.broadcasted_iota(I32, s.shape, 0)
    rank = jnp.full(s.shape, n_top, I32)
    vals, idxs = [], []
    for r in range(n_top):
        m = jnp.max(s, axis=0, keepdims=True)
        idx = jnp.min(jnp.where(s == m, row_id, rows), axis=0, keepdims=True)
        hit = row_id == idx
        vals.append(m)
        idxs.append(idx)
        rank = jnp.where(hit, r, rank)
        s = jnp.where(hit, -jnp.inf, s)
    return vals, idxs, rank


def _peer_kernel(x_ref, mod_ref, wq_ref, kt_ref, u_ref, vt_ref, g_ref, b_ref, y_ref,
                 h2_sc, st_sc, e0_sc, nb_sc, e1_sc, r1_sc, acc_sc, *, alpha, n_keys):
    ks, r, d = x_ref.shape
    tm = ks * r
    step = pl.program_id(1)
    sub = PEER_EXPERT_TILE // n_keys

    @pl.when(step == 0)
    def _():
        h2 = (x_ref[...] * (1.0 + mod_ref[4]) + mod_ref[3]).reshape(tm, d).astype(BF16)
        h2_sc[...] = h2
        qp = _dot(h2, wq_ref[...]).astype(BF16)
        st_sc[...] = _dot_nt(kt_ref[...], qp)
        acc_sc[...] = jnp.zeros(acc_sc.shape, F32)

        def route(h, carry):
            s0 = st_sc[pl.ds(pl.multiple_of(h * 2 * n_keys, 2 * n_keys), n_keys), :]
            s1 = st_sc[pl.ds(pl.multiple_of(h * 2 * n_keys + n_keys, n_keys), n_keys), :]
            v0, _, rank0 = _top_rows(s0, PEER_TOPK)
            v1, _, rank1 = _top_rows(s1, PEER_TOPK)
            v1m = jnp.concatenate(v1, axis=0)
            cand = jnp.concatenate([v0[a] + v1m for a in range(PEER_TOPK)], axis=0)
            top_s, top_p, _ = _top_rows(cand, PEER_TOPK)
            z = jnp.zeros_like(top_s[0])
            for k in range(PEER_TOPK):
                z = z + jnp.exp(top_s[k] - top_s[0])
            first = [p >> int(math.log2(PEER_TOPK)) for p in top_p]
            take = jnp.zeros(rank0.shape, I32)
            for a in range(PEER_TOPK):
                n_a = sum(jnp.where(f == a, 1, 0) for f in first)
                take = jnp.where(rank0 == a, n_a, take)
            e0_sc[h] = (jnp.exp(s0 - v0[0]) / z).reshape(n_keys // SUBLANES, SUBLANES, tm)
            nb_sc[h] = take.reshape(n_keys // SUBLANES, SUBLANES, tm)
            e1_sc[h] = jnp.exp(s1 - v1[0])
            r1_sc[h] = rank1
            return carry

        lax.fori_loop(0, PEER_HEADS, route, 0)

    h2 = h2_sc[...]
    tile = step * sub // SUBLANES
    for half in range(sub // 2):
        parts = []
        for ii in (2 * half, 2 * half + 1):
            act = _dot_nt(u_ref[ii * n_keys:(ii + 1) * n_keys, :], h2)
            gate = jnp.zeros((n_keys, tm), F32)
            for h in range(PEER_HEADS):
                take_i = nb_sc[h, tile, ii % SUBLANES:ii % SUBLANES + 1, :]
                e0_i = e0_sc[h, tile, ii % SUBLANES:ii % SUBLANES + 1, :]
                gate = gate + jnp.where(r1_sc[h] < take_i, e1_sc[h], 0.0) * e0_i
            gelu = 0.5 * act * (1.0 + lax.erf(act * (2.0 ** -0.5)))
            parts.append((gelu * gate).astype(BF16))
        w = jnp.concatenate(parts, axis=0)
        acc_sc[...] += _dot(vt_ref[:, 2 * half * n_keys:(2 * half + 2) * n_keys], w)

    @pl.when(step == pl.num_programs(1) - 1)
    def _():
        f = acc_sc[...].T.reshape(ks, r, d)
        y = alpha * x_ref[...] + mod_ref[5] * f
        y_ref[...] = _layer_norm(y, g_ref[...], b_ref[...])


def _peer(x3, mod4, ks, wq, kt, u_b, vt_b, ln_g, ln_b, alpha, n_keys):
    g, r, d = x3.shape
    tm = ks * r
    n_exp = u_b.shape[0]
    assert PEER_EXPERT_TILE % (SUBLANES * n_keys) == 0 and n_exp % PEER_EXPERT_TILE == 0
    full = lambda arr: pl.BlockSpec(arr.shape, lambda i, e: (0,) * arr.ndim)
    return pl.pallas_call(
        functools.partial(_peer_kernel, alpha=alpha, n_keys=n_keys),
        grid=(g // ks, n_exp // PEER_EXPERT_TILE),
        in_specs=[pl.BlockSpec((ks, r, d), lambda i, e: (i, 0, 0)),
                  pl.BlockSpec((N_ADA, ks, 1, d), lambda i, e: (0, i, 0, 0)),
                  full(wq), full(kt),
                  pl.BlockSpec((PEER_EXPERT_TILE, d), lambda i, e: (e, 0)),
                  pl.BlockSpec((d, PEER_EXPERT_TILE), lambda i, e: (0, e)),
                  full(ln_g), full(ln_b)],
        out_specs=pl.BlockSpec((ks, r, d), lambda i, e: (i, 0, 0)),
        out_shape=jax.ShapeDtypeStruct((g, r, d), F32),
        scratch_shapes=[pltpu.VMEM((tm, d), BF16),
                        pltpu.VMEM((kt.shape[0], tm), F32),
                        pltpu.VMEM((PEER_HEADS, n_keys // SUBLANES, SUBLANES, tm), F32),
                        pltpu.VMEM((PEER_HEADS, n_keys // SUBLANES, SUBLANES, tm), I32),
                        pltpu.VMEM((PEER_HEADS, n_keys, tm), F32),
                        pltpu.VMEM((PEER_HEADS, n_keys, tm), I32),
                        pltpu.VMEM((d, tm), F32)],
        compiler_params=_cparams(("arbitrary", "arbitrary")),
        name="peer_ln2",
    )(x3, mod4, wq, kt, u_b, vt_b, ln_g, ln_b)


def _layer_weights(w_in, pool_w, w_branch_a, w_branch_b, w_out, peer_wq, peer_sub_keys, peer_u, peer_v,
                   *, pw, aw, iw, idim):
    d = w_in.shape[0]
    o = np.cumsum([0, pw, aw, aw, aw, iw, idim, IDX_HEADS, d, d])
    cut = lambda j: w_in[:, o[j]:o[j + 1]]
    u_w, q_w, k_w, v_w, qi_w, ki_w, wi_w, ga_w, gb_w = (cut(j) for j in range(9))
    wn = jnp.concatenate([u_w, k_w, v_w], axis=1).astype(BF16)
    wg = jnp.concatenate([ga_w, gb_w], axis=1).astype(BF16)
    wi_pad = jnp.pad(wi_w, ((0, 0), (0, 2 * SUBLANES - IDX_HEADS)))
    wt = jnp.concatenate([v_w, q_w, qi_w, wi_pad], axis=1).T.astype(BF16)
    n_keys, half = peer_sub_keys.shape[1], peer_sub_keys.shape[2]
    kt = jnp.einsum("hg,cb,ckd->hckgbd", jnp.eye(PEER_HEADS, dtype=F32), jnp.eye(2, dtype=F32), peer_sub_keys)
    kt = kt.reshape(PEER_HEADS * 2 * n_keys, PEER_HEADS * 2 * half).astype(BF16)
    return dict(wn=wn, wki=ki_w.astype(BF16), wg=wg, wt=wt, pool_w=pool_w.astype(BF16),
                wa=w_branch_a.astype(BF16), wb=w_branch_b.astype(BF16), wo=w_out.astype(BF16),
                wq=peer_wq.astype(BF16), kt=kt, u_b=peer_u.astype(BF16), vt_b=peer_v.T.astype(BF16))


def kernel(x_prompt, x_sample, cache_k, cache_v, cache_kidx, state_pool, page_table, c_prompt, c_sample,
           w_ada, b_ada, w_in, pool_w, pool_scale, w_branch_a, w_branch_b, w_out, ln1_g, ln1_b,
           peer_wq, peer_sub_keys, peer_u, peer_v, ln2_g, ln2_b):
    depth = w_ada.shape[0]
    b, s, d = x_prompt.shape
    db, t_new, _ = x_sample.shape
    page, n_heads, hd = cache_k.shape[2:]
    idim = cache_kidx.shape[3]
    pool_hist, pw = state_pool.shape[2:]
    n_pages = page_table.shape[1]
    past = n_pages * page
    aw, iw = n_heads * hd, IDX_HEADS * idim
    n_keys = peer_sub_keys.shape[2]
    alpha = (2 * depth) ** 0.25
    idx_scale = float(iw) ** -0.5
    q_scale = float(hd) ** -0.5
    assert t_new == SUBLANES and n_heads == SUBLANES and pool_hist < HIST_ROWS and s % TOKEN_TILE == 0
    assert q_scale == 2.0 ** round(math.log2(q_scale))

    n_c = b + db
    c_all = jnp.pad(jnp.concatenate([c_prompt, c_sample], axis=0), ((0, -n_c % SUBLANES), (0, 0)))
    y_p, y_s = x_prompt, x_sample
    outs = [[] for _ in range(8)]
    for l in range(depth):
        w = _layer_weights(w_in[l], pool_w[l], w_branch_a[l], w_branch_b[l], w_out[l], peer_wq[l], peer_sub_keys[l],
                           peer_u[l], peer_v[l], pw=pw, aw=aw, iw=iw, idim=idim)
        mod = _ada(c_all, w_ada[l], b_ada[l])
        mod_p, mod_s = mod[:, :b], mod[:, b:n_c]
        ps, g1, b1, g2, b2 = pool_scale[l][None], ln1_g[l][None], ln1_b[l][None], ln2_g[l][None], ln2_b[l][None]
        proj = functools.partial(_proj, wn=w["wn"], wki=w["wki"], wg=w["wg"], wt=w["wt"],
                                 pw=pw, aw=aw, iw=iw, idim=idim, q_scale=q_scale)

        x3, mod4, ks = _grouped(y_p, mod_p, TOKEN_TILE)
        u, k, v, ki, kb, kib, vt, qt, qit, wit, ga, gb = proj(x3, mod4, ks)
        nck = s // TOKEN_TILE
        a = _pool_prompt(u.reshape(b, s, pw), w["pool_w"], ps)
        o = _attn_prompt(qt.reshape(b, nck, aw, TOKEN_TILE), qit.reshape(b, nck, iw, TOKEN_TILE),
                         wit.reshape(b, nck, 2 * SUBLANES, TOKEN_TILE), kb.reshape(b, nck, TOKEN_TILE, aw),
                         vt.reshape(b, nck, aw, TOKEN_TILE), kib.reshape(b, nck, TOKEN_TILE, idim),
                         n_sel=min(TOPK_MAX, s // 4), idx_scale=idx_scale, n_heads=n_heads, hd=hd, idim=idim)
        x1 = _merge(x3, mod4, ks, a.reshape(b * s, pw), o.reshape(b * s, aw), ga, gb,
                    w["wa"], w["wb"], w["wo"], g1, b1, alpha)
        x3p, mod4p, ksp = _grouped(x1.reshape(b, s, d), mod_p, PEER_TOKEN_TILE)
        y_p = _peer(x3p, mod4p, ksp, w["wq"], w["kt"], w["u_b"], w["vt_b"], g2, b2, alpha, n_keys).reshape(b, s, d)
        outs[0].append(k.reshape(b, s, n_heads, hd))
        outs[1].append(v.reshape(b, s, n_heads, hd))
        outs[2].append(ki.reshape(b, s, idim))
        outs[3].append(u.reshape(b, s, pw)[:, s - pool_hist:])

        x3, mod4, ks = _grouped(y_s, mod_s, TOKEN_TILE)
        u, k, v, ki, kb, kib, vt, qt, qit, wit, ga, gb = proj(x3, mod4, ks)
        hist = state_pool[l]
        hist16 = jnp.pad(hist, ((0, 0), (HIST_ROWS - pool_hist, 0), (0, 0)))
        a = _pool_sample(u.reshape(db, t_new, pw), hist16, w["pool_w"], ps, past)
        untile = lambda z: jnp.moveaxis(z, 0, 1).reshape(z.shape[1], db, t_new)
        heads_first = lambda z, nh: jnp.transpose(z.reshape(nh, -1, db, t_new), (2, 0, 3, 1))
        q_s = heads_first(untile(qt), n_heads)
        qi_s = heads_first(untile(qit), IDX_HEADS)
        wi_s = jnp.transpose(untile(wit)[:IDX_HEADS], (1, 0, 2))[..., None]
        new_rows = -(-t_new * n_heads // LANES) * LANES
        pad_rows = lambda z: jnp.pad(z, ((0, 0), (0, new_rows - z.shape[1]), (0, 0)))
        knew = pad_rows(kb.reshape(db, t_new * n_heads, hd))
        vnew = pad_rows(v.astype(BF16).reshape(db, t_new * n_heads, hd))
        kinew = jnp.pad(kib.reshape(db, t_new, idim), ((0, 0), (0, page - t_new), (0, 0)))
        o = _attn_sample(page_table, q_s, qi_s, wi_s, knew, vnew, kinew, cache_k, cache_v, cache_kidx, l,
                         n_sel=min(TOPK_MAX, (past + t_new) // 4), idx_scale=idx_scale)
        o = jnp.transpose(o, (0, 2, 1, 3)).reshape(db * t_new, aw).astype(BF16)
        x1 = _merge(x3, mod4, ks, a, o, ga, gb, w["wa"], w["wb"], w["wo"], g1, b1, alpha)
        x3p, mod4p, ksp = _grouped(x1, mod_s, PEER_TOKEN_TILE)
        y_s = _peer(x3p, mod4p, ksp, w["wq"], w["kt"], w["u_b"], w["vt_b"], g2, b2, alpha, n_keys)
        outs[4].append(k.reshape(db, t_new, n_heads, hd))
        outs[5].append(v.reshape(db, t_new, n_heads, hd))
        outs[6].append(ki.reshape(db, t_new, idim))
        outs[7].append(jnp.concatenate([hist, u.reshape(db, t_new, pw)], axis=1)[:, t_new:])
    return (y_p, y_s) + tuple(jnp.stack(o_) for o_ in outs)
```

```python
import functools
import math

import jax
import jax.numpy as jnp
import numpy as np
from jax import lax
from jax.experimental import pallas as pl
from jax.experimental.pallas import tpu as pltpu

F32, BF16, I32 = jnp.float32, jnp.bfloat16, jnp.int32

POOL_WINDOWS = (2, 4, 8, 16)
IDX_HEADS = 8
TOPK_MAX = 256
PEER_HEADS = 8
PEER_TOPK = 16
LN_EPS = 1e-5
N_ADA = 6

VMEM_LIMIT_BYTES = 56 * 1024 * 1024
SUBLANES = 8
LANES = 128
TOKEN_TILE = 256
PEER_TOKEN_TILE = 512
PEER_EXPERT_TILE = 1024
PEER_TOKEN_SPLIT = 2
HIST_ROWS = 16

INT_MIN = -(2 ** 31)
KEY_NEG_INF = -2139095041
NEG_BIG = -1e30


def _cparams(sem):
    return pltpu.CompilerParams(dimension_semantics=sem, vmem_limit_bytes=VMEM_LIMIT_BYTES)


def _dot(a, b):
    return jnp.dot(a, b, preferred_element_type=F32)


def _dot_nt(a, b):
    return lax.dot_general(a, b, (((1,), (1,)), ((), ())), preferred_element_type=F32)


def _key_to_float(key):
    return lax.bitcast_convert_type(key ^ ((key >> 31) & 0x7FFFFFFF), F32)


def _threshold_step(i, key, count_ge, n_sel):
    trial = key + lax.shift_left(jnp.int32(1), 31 - i)
    ok = jnp.logical_or(count_ge(_key_to_float(trial)) >= n_sel, trial < KEY_NEG_INF)
    return jnp.where(ok, trial, key)


def _ada_kernel(c_ref, w_ref, b_ref, o_ref):
    o_ref[...] = _dot(c_ref[...].astype(BF16), w_ref[...].astype(BF16)) + b_ref[...]


def _ada(c_all, w_ada, b_ada):
    n, d = c_all.shape
    return pl.pallas_call(
        _ada_kernel,
        grid=(N_ADA,),
        in_specs=[pl.BlockSpec((n, d), lambda j: (0, 0)),
                  pl.BlockSpec((d, d), lambda j: (0, j)),
                  pl.BlockSpec((1, d), lambda j: (0, j))],
        out_specs=pl.BlockSpec((None, n, d), lambda j: (j, 0, 0)),
        out_shape=jax.ShapeDtypeStruct((N_ADA, n, d), F32),
        compiler_params=_cparams(("arbitrary",)),
        name="ada",
    )(c_all, w_ada, b_ada.reshape(1, -1))


def _grouped(x, mod, rows):
    nseq, l, d = x.shape
    if l >= rows:
        assert l % rows == 0
        per = l // rows
        x3 = x.reshape(nseq * per, rows, d)
        mod4 = jnp.broadcast_to(mod[:, :, None, None, :], (N_ADA, nseq, per, 1, d)).reshape(N_ADA, nseq * per, 1, d)
        return x3, mod4, 1
    assert rows % l == 0 and nseq % (rows // l) == 0
    return x, mod[:, :, None, :], rows // l


def _proj_kernel(x_ref, mod_ref, wn_ref, wki_ref, wg_ref, wt_ref,
                 u_ref, k_ref, v_ref, ki_ref, kb_ref, kib_ref, vt_ref, qt_ref, qit_ref, wit_ref, ga_ref, gb_ref,
                 *, pw, aw, iw, q_scale):
    ks, r, d = x_ref.shape
    h = (x_ref[...] * (1.0 + mod_ref[1]) + mod_ref[0]).reshape(ks * r, d).astype(BF16)
    u_ref[...] = _dot(h, wn_ref[:, 0:pw])
    k = _dot(h, wn_ref[:, pw:pw + aw])
    k_ref[...] = k
    kb_ref[...] = k.astype(BF16)
    v_ref[...] = _dot(h, wn_ref[:, pw + aw:pw + 2 * aw])
    ki = _dot(h, wki_ref[...])
    ki_ref[...] = ki
    kib_ref[...] = ki.astype(BF16)
    ga_ref[...] = _dot(h, wg_ref[:, 0:d])
    gb_ref[...] = _dot(h, wg_ref[:, d:2 * d])
    vt_ref[...] = _dot_nt(wt_ref[0:aw, :], h).astype(BF16)
    qt_ref[...] = (_dot_nt(wt_ref[aw:2 * aw, :], h) * q_scale).astype(BF16)
    qit_ref[...] = _dot_nt(wt_ref[2 * aw:2 * aw + iw, :], h).astype(BF16)
    wit_ref[...] = _dot_nt(wt_ref[2 * aw + iw:2 * aw + iw + 2 * SUBLANES, :], h)


def _proj(x3, mod4, ks, wn, wki, wg, wt, *, pw, aw, iw, idim, q_scale):
    g, r, d = x3.shape
    tm = ks * r
    nb = g // ks
    n = g * r
    row = lambda cols: pl.BlockSpec((tm, cols), lambda i: (i, 0))
    col = lambda rows: pl.BlockSpec((None, rows, tm), lambda i: (i, 0, 0))
    full = lambda a: pl.BlockSpec(a.shape, lambda i: (0,) * a.ndim)
    outs = [((n, pw), F32, row(pw)), ((n, aw), F32, row(aw)), ((n, aw), F32, row(aw)), ((n, idim), F32, row(idim)),
            ((n, aw), BF16, row(aw)), ((n, idim), BF16, row(idim)),
            ((nb, aw, tm), BF16, col(aw)), ((nb, aw, tm), BF16, col(aw)), ((nb, iw, tm), BF16, col(iw)),
            ((nb, 2 * SUBLANES, tm), F32, col(2 * SUBLANES)),
            ((n, d), F32, row(d)), ((n, d), F32, row(d))]
    return pl.pallas_call(
        functools.partial(_proj_kernel, pw=pw, aw=aw, iw=iw, q_scale=q_scale),
        grid=(nb,),
        in_specs=[pl.BlockSpec((ks, r, d), lambda i: (i, 0, 0)),
                  pl.BlockSpec((N_ADA, ks, 1, d), lambda i: (0, i, 0, 0)),
                  full(wn), full(wki), full(wg), full(wt)],
        out_specs=[o[2] for o in outs],
        out_shape=[jax.ShapeDtypeStruct(o[0], o[1]) for o in outs],
        compiler_params=_cparams(("arbitrary",)),
        name="in_proj",
    )(x3, mod4, wn, wki, wg, wt)


def _pool_windows(ext_ref, pos, pw_ref, ps_ref, a_ref):
    ks, rows, width = ext_ref.shape
    r = rows - HIST_ROWS
    gw = width // len(POOL_WINDOWS)
    for g, w in enumerate(POOL_WINDOWS):
        lo = g * gw
        cur = ext_ref[:, HIST_ROWS:HIST_ROWS + r, lo:lo + gw]
        acc = cur
        for j in range(1, w):
            acc = acc + ext_ref[:, HIST_ROWS - j:HIST_ROWS - j + r, lo:lo + gw]
        cnt = jnp.minimum(pos + 1, w).astype(F32)
        pooled = (acc / cnt - cur).reshape(ks * r, gw).astype(BF16)
        mixed = _dot(pooled, pw_ref[g])
        a_ref[:, lo:lo + gw] = (mixed * ps_ref[:, lo:lo + gw]).astype(BF16)


def _pool_prompt_kernel(u_ref, pw_ref, ps_ref, a_ref, ext_ref):
    i = pl.program_id(1)
    tp, width = u_ref.shape
    gw = width // len(POOL_WINDOWS)

    @pl.when(i == 0)
    def _():
        ext_ref[:, 0:HIST_ROWS, :] = jnp.zeros((1, HIST_ROWS, width), F32)

    @pl.when(i > 0)
    def _():
        ext_ref[:, 0:HIST_ROWS, :] = ext_ref[:, tp:tp + HIST_ROWS, :]

    ext_ref[:, HIST_ROWS:, :] = u_ref[...][None]
    pos = i * tp + lax.broadcasted_iota(I32, (1, tp, gw), 1)
    _pool_windows(ext_ref, pos, pw_ref, ps_ref, a_ref)


def _pool_prompt(u, pool_w, pool_scale):
    b, s, width = u.shape
    tp = TOKEN_TILE
    return pl.pallas_call(
        _pool_prompt_kernel,
        grid=(b, s // tp),
        in_specs=[pl.BlockSpec((None, tp, width), lambda bi, i: (bi, i, 0)),
                  pl.BlockSpec(pool_w.shape, lambda bi, i: (0, 0, 0)),
                  pl.BlockSpec(pool_scale.shape, lambda bi, i: (0, 0))],
        out_specs=pl.BlockSpec((None, tp, width), lambda bi, i: (bi, i, 0)),
        out_shape=jax.ShapeDtypeStruct((b, s, width), BF16),
        scratch_shapes=[pltpu.VMEM((1, HIST_ROWS + tp, width), F32)],
        compiler_params=_cparams(("arbitrary", "arbitrary")),
        name="pool_prompt",
    )(u, pool_w, pool_scale)


def _pool_sample_kernel(u_ref, hist_ref, pw_ref, ps_ref, a_ref, ext_ref, *, pos0):
    ks, t, width = u_ref.shape
    gw = width // len(POOL_WINDOWS)
    ext_ref[:, 0:HIST_ROWS, :] = hist_ref[...]
    ext_ref[:, HIST_ROWS:, :] = u_ref[...]
    pos = pos0 + lax.broadcasted_iota(I32, (1, t, gw), 1)
    _pool_windows(ext_ref, pos, pw_ref, ps_ref, a_ref)


def _pool_sample(u, hist16, pool_w, pool_scale, pos0):
    db, t, width = u.shape
    ks = TOKEN_TILE // t
    return pl.pallas_call(
        functools.partial(_pool_sample_kernel, pos0=pos0),
        grid=(db // ks,),
        in_specs=[pl.BlockSpec((ks, t, width), lambda i: (i, 0, 0)),
                  pl.BlockSpec((ks, HIST_ROWS, width), lambda i: (i, 0, 0)),
                  pl.BlockSpec(pool_w.shape, lambda i: (0, 0, 0)),
                  pl.BlockSpec(pool_scale.shape, lambda i: (0, 0))],
        out_specs=pl.BlockSpec((ks * t, width), lambda i: (i, 0)),
        out_shape=jax.ShapeDtypeStruct((db * t, width), BF16),
        scratch_shapes=[pltpu.VMEM((ks, HIST_ROWS + t, width), F32)],
        compiler_params=_cparams(("arbitrary",)),
        name="pool_sample",
    )(u, hist16, pool_w, pool_scale)


def _alibi_slopes(n_heads):
    return [2.0 ** (-8.0 * (h + 1) / n_heads) for h in range(n_heads)]


def _attn_prompt_kernel(qt_ref, qit_ref, wit_ref, kb_ref, vt_ref, kib_ref, o_ref,
                        sc_sc, qz_sc, acc_sc, lg_sc, *, n_sel, idx_scale, n_heads, hd, idim, seq_len):
    qb = pl.program_id(1)
    tq = qt_ref.shape[-1]
    sk = tq
    nchunk = qb + 1
    t_idx = qb * tq + lax.broadcasted_iota(I32, (sk, tq), 1)
    s_loc = lax.broadcasted_iota(I32, (sk, tq), 0)

    def score_body(c, carry):
        kic = kib_ref[c]
        acc = jnp.zeros((sk, tq), F32)
        for h in range(IDX_HEADS):
            rel = _dot(kic, qit_ref[h * idim:(h + 1) * idim, :])
            acc = acc + wit_ref[h:h + 1, :] * jnp.maximum(rel, 0.0)
        sc_sc[c] = jnp.where(c * sk + s_loc <= t_idx, acc * idx_scale, -jnp.inf)
        return carry

    lax.fori_loop(0, nchunk, score_body, 0)

    def colsum(mask):
        return jnp.where(mask, 1.0, 0.0).reshape(sk // SUBLANES, SUBLANES, tq).sum(axis=0)

    def count(pred):
        acc = lax.fori_loop(0, nchunk, lambda c, a: a + colsum(pred(c, sc_sc[c])), jnp.zeros((SUBLANES, tq), F32))
        return acc.sum(axis=0, keepdims=True)

    count_ge = lambda t: count(lambda c, s: s >= t)
    tau_key = lax.fori_loop(0, 32, lambda i, k: _threshold_step(i, k, count_ge, n_sel),
                            jnp.full((1, tq), INT_MIN, I32))
    tau = _key_to_float(tau_key)

    tie_rows = jnp.logical_and(count_ge(tau) > n_sel, tau_key > KEY_NEG_INF)
    has_tie = jnp.max(jnp.where(tie_rows, 1.0, 0.0)) > 0.5

    @pl.when(has_tie)
    def _():
        need = n_sel - count(lambda c, s: s > tau)
        nbits = max(1, int(math.ceil(math.log2(seq_len))))

        def first_enough(i, j):
            step = lax.shift_left(jnp.int32(1), nbits - 1 - i)
            upto = count(lambda c, s: jnp.logical_and(s == tau, c * sk + s_loc <= j + step - 1))
            return jnp.where(upto < need, j + step, j)

        last = lax.fori_loop(0, nbits, first_enough, jnp.zeros((1, tq), I32))

        def demote(c, carry):
            s = sc_sc[c]
            drop = jnp.logical_and(jnp.logical_and(tie_rows, s == tau), c * sk + s_loc > last)
            sc_sc[c] = jnp.where(drop, -jnp.inf, s)
            return carry

        lax.fori_loop(0, nchunk, demote, 0)

    pair_rows = lax.broadcasted_iota(I32, (2 * hd, tq), 0)
    for h in range(n_heads):
        blk = qt_ref[(h // 2) * 2 * hd:(h // 2 + 1) * 2 * hd, :]
        keep = pair_rows < hd if h % 2 == 0 else pair_rows >= hd
        qz_sc[h] = jnp.where(keep, blk, jnp.zeros_like(blk))
    acc_sc[...] = jnp.zeros(acc_sc.shape, F32)
    slopes = _alibi_slopes(n_heads)

    def att_body(c, carry):
        m_all, l_all = carry
        pos = c * sk + s_loc
        sel = jnp.logical_and(sc_sc[c] >= tau, pos <= t_idx)
        dist = (pos - qb * tq).astype(F32)
        for h in range(n_heads):
            lg_sc[h] = _dot(kb_ref[c, :, (h // 2) * 2 * hd:(h // 2 + 1) * 2 * hd], qz_sc[h])
        m_out, l_out = [], []
        for h in range(n_heads):
            lg = jnp.where(sel, lg_sc[h] + slopes[h] * dist, -jnp.inf)
            m_new = jnp.maximum(m_all[h], jnp.max(lg, axis=0, keepdims=True))
            alpha = jnp.exp(m_all[h] - m_new)
            p = jnp.exp(lg - m_new)
            l_out.append(alpha * l_all[h] + jnp.sum(p, axis=0, keepdims=True))
            m_out.append(m_new)
            pv = _dot(vt_ref[c, h * hd:(h + 1) * hd, :], p.astype(BF16))
            acc_sc[h * hd:(h + 1) * hd, :] = alpha * acc_sc[h * hd:(h + 1) * hd, :] + pv
        return tuple(m_out), tuple(l_out)

    init = (tuple(jnp.full((1, tq), NEG_BIG, F32) for _ in range(n_heads)),
            tuple(jnp.zeros((1, tq), F32) for _ in range(n_heads)))
    _, l_all = lax.fori_loop(0, nchunk, att_body, init)
    out_t = jnp.concatenate([acc_sc[h * hd:(h + 1) * hd, :] / l_all[h] for h in range(n_heads)], axis=0)
    o_ref[...] = out_t.T.astype(BF16)


def _attn_prompt(qt, qit, wit, kb, vt, kib, *, n_sel, idx_scale, n_heads, hd, idim):
    b, nck, aw, tq = qt.shape
    iw = qit.shape[2]
    resident = lambda shape: pl.BlockSpec((None,) + shape, lambda bi, i: (bi, 0, 0, 0), pipeline_mode=pl.Buffered(1))
    tile = lambda rows: pl.BlockSpec((None, None, rows, tq), lambda bi, i: (bi, i, 0, 0))
    return pl.pallas_call(
        functools.partial(_attn_prompt_kernel, n_sel=n_sel, idx_scale=idx_scale, n_heads=n_heads, hd=hd, idim=idim,
                          seq_len=nck * tq),
        grid=(b, nck),
        in_specs=[tile(aw), tile(iw), tile(2 * SUBLANES),
                  resident((nck, tq, aw)), resident((nck, aw, tq)), resident((nck, tq, idim))],
        out_specs=pl.BlockSpec((None, tq, aw), lambda bi, i: (bi, i, 0)),
        out_shape=jax.ShapeDtypeStruct((b, nck * tq, aw), BF16),
        scratch_shapes=[pltpu.VMEM((nck, tq, tq), F32),
                        pltpu.VMEM((n_heads, 2 * hd, tq), BF16),
                        pltpu.VMEM((aw, tq), F32),
                        pltpu.VMEM((n_heads, tq, tq), F32)],
        compiler_params=_cparams(("arbitrary", "arbitrary")),
        name="attn_prompt",
    )(qt, qit, wit, kb, vt, kib)


def _attn_sample_kernel(pt_ref, qbd_ref, qi_ref, wi_ref, knew_ref, vnew_ref, kinew_ref, *rest,
                        n_pages, page, n_sel, idx_scale, n_heads, hd, t_new):
    del pt_ref
    ki_refs, k_refs, v_refs = rest[0:n_pages], rest[n_pages:2 * n_pages], rest[2 * n_pages:3 * n_pages]
    o_ref, sc_sc, lg_sc = rest[3 * n_pages:]
    past = n_pages * page
    width = past + page
    group = n_heads // 2
    row = lax.broadcasted_iota(I32, (t_new, page), 0)
    lane = lax.broadcasted_iota(I32, (t_new, page), 1)
    wi = wi_ref[...]

    def scores(kit):
        rel = jnp.maximum(_dot(qi_ref[...], kit), 0.0)
        return jnp.sum(rel.reshape(IDX_HEADS, t_new, rel.shape[1]) * wi, axis=0) * idx_scale

    def page_pair(refs, pp, head_lo=None):
        if head_lo is None:
            parts = [refs[2 * pp + j][...] for j in range(2)]
        else:
            parts = [refs[2 * pp + j][head_lo:head_lo + group].reshape(group * hd, page) for j in range(2)]
        return jnp.concatenate(parts, axis=1).astype(BF16)

    for pp in range(n_pages // 2):
        sc_sc[:, 2 * pp * page:(2 * pp + 2) * page] = scores(page_pair(ki_refs, pp))
    sc_sc[:, past:width] = jnp.where(lane <= row, scores(kinew_ref[...]), -jnp.inf)

    def count(pred):
        return jnp.sum(jnp.where(pred(sc_sc[...]), 1.0, 0.0), axis=1, keepdims=True)

    count_ge = lambda t: count(lambda s: s >= t)
    tau_key = lax.fori_loop(0, 32, lambda i, k: _threshold_step(i, k, count_ge, n_sel),
                            jnp.full((t_new, 1), INT_MIN, I32))
    tau = _key_to_float(tau_key)

    tie_rows = jnp.logical_and(count_ge(tau) > n_sel, tau_key > KEY_NEG_INF)
    has_tie = jnp.max(jnp.where(tie_rows, 1.0, 0.0)) > 0.5
    pos = lax.broadcasted_iota(I32, (t_new, width), 1)

    @pl.when(has_tie)
    def _():
        need = n_sel - count(lambda s: s > tau)
        nbits = max(1, int(math.ceil(math.log2(width))))

        def first_enough(i, j):
            step = lax.shift_left(jnp.int32(1), nbits - 1 - i)
            upto = count(lambda s: jnp.logical_and(s == tau, pos <= j + step - 1))
            return jnp.where(upto < need, j + step, j)

        last = lax.fori_loop(0, nbits, first_enough, jnp.zeros((t_new, 1), I32))
        s = sc_sc[...]
        drop = jnp.logical_and(jnp.logical_and(tie_rows, s == tau), pos > last)
        sc_sc[...] = jnp.where(drop, -jnp.inf, s)

    slopes = _alibi_slopes(n_heads)
    t_pos = past + lax.broadcasted_iota(I32, (t_new, width), 0)
    sel = jnp.logical_and(sc_sc[...] >= tau, pos <= t_pos)
    bias = jnp.where(sel, 0.0, -jnp.inf)
    dist = (t_pos - pos).astype(F32)
    spans = [(2 * pp * page, 2 * page) for pp in range(n_pages // 2)] + [(past, page)]
    for g in range(2):
        for pp, (lo, n) in enumerate(spans):
            kt = page_pair(k_refs, pp, g * group) if pp < n_pages // 2 else knew_ref[g]
            lg = _dot(qbd_ref[g], kt).reshape(group, t_new, n)
            for hl in range(group):
                h = g * group + hl
                lg_sc[h, :, lo:lo + n] = lg[hl] - slopes[h] * dist[:, lo:lo + n] + bias[:, lo:lo + n]
    for g in range(2):
        probs, denoms = [], []
        for hl in range(group):
            lg = lg_sc[g * group + hl]
            pr = jnp.exp(lg - jnp.max(lg, axis=1, keepdims=True))
            denoms.append(jnp.sum(pr, axis=1, keepdims=True))
            probs.append(pr)
        pg = jnp.concatenate(probs, axis=0).astype(BF16)
        acc = jnp.zeros((group * t_new, group * hd), F32)
        for pp, (lo, n) in enumerate(spans):
            vt = page_pair(v_refs, pp, g * group) if pp < n_pages // 2 else vnew_ref[g]
            acc = acc + _dot_nt(pg[:, lo:lo + n], vt)
        for hl in range(group):
            o_ref[g * group + hl] = acc[hl * t_new:(hl + 1) * t_new, hl * hd:(hl + 1) * hd] / denoms[hl]


def _attn_sample(page_table, qbd, qi2, wi3, knew, vnew, kinew, k_t, v_t, ki_t, layer, *, n_sel, idx_scale):
    db = qbd.shape[0]
    n_pages = page_table.shape[1]
    n_heads, hd, page = k_t.shape[2:]
    idim = ki_t.shape[2]
    t_new = wi3.shape[2]
    assert n_pages % 2 == 0 and n_heads % 2 == 0
    per_seq = lambda a: pl.BlockSpec((None,) + a.shape[1:], lambda b, pt: (b,) + (0,) * (a.ndim - 1))
    kv_page = lambda j: pl.BlockSpec((None, None, n_heads, hd, page), lambda b, pt: (layer, pt[b, j], 0, 0, 0))
    ki_page = lambda j: pl.BlockSpec((None, None, idim, page), lambda b, pt: (layer, pt[b, j], 0, 0))
    in_specs = ([per_seq(qbd), per_seq(qi2), per_seq(wi3), per_seq(knew), per_seq(vnew), per_seq(kinew)]
                + [ki_page(j) for j in range(n_pages)]
                + [kv_page(j) for j in range(n_pages)]
                + [kv_page(j) for j in range(n_pages)])
    grid_spec = pltpu.PrefetchScalarGridSpec(
        num_scalar_prefetch=1,
        grid=(db,),
        in_specs=in_specs,
        out_specs=pl.BlockSpec((None, n_heads, t_new, hd), lambda b, pt: (b, 0, 0, 0)),
        scratch_shapes=[pltpu.VMEM((t_new, (n_pages + 1) * page), F32),
                        pltpu.VMEM((n_heads, t_new, (n_pages + 1) * page), F32)],
    )
    return pl.pallas_call(
        functools.partial(_attn_sample_kernel, n_pages=n_pages, page=page, n_sel=n_sel, idx_scale=idx_scale,
                          n_heads=n_heads, hd=hd, t_new=t_new),
        grid_spec=grid_spec,
        out_shape=jax.ShapeDtypeStruct((db, n_heads, t_new, hd), F32),
        compiler_params=_cparams(("arbitrary",)),
        name="attn_sample",
    )(page_table, qbd, qi2, wi3, knew, vnew, kinew, *([ki_t] * n_pages), *([k_t] * n_pages), *([v_t] * n_pages))


def _layer_norm(y, g, b):
    mu = jnp.mean(y, axis=-1, keepdims=True)
    var = jnp.mean(jnp.square(y - mu), axis=-1, keepdims=True)
    return (y - mu) * lax.rsqrt(var + LN_EPS) * g + b


def _merge_kernel(x_ref, mod_ref, a_ref, o_ref, ga_ref, gb_ref, wa_ref, wb_ref, wo_ref, g_ref, b_ref, x1_ref, *, alpha):
    ks, r, d = x_ref.shape
    merged = (jax.nn.sigmoid(ga_ref[...]) * _dot(a_ref[...], wa_ref[...])
              + jax.nn.sigmoid(gb_ref[...]) * _dot(o_ref[...], wb_ref[...]))
    upd = _dot(merged.astype(BF16), wo_ref[...]).reshape(ks, r, d)
    y = alpha * x_ref[...] + mod_ref[2] * upd
    x1_ref[...] = _layer_norm(y, g_ref[...], b_ref[...])


def _merge(x3, mod4, ks, a, o, ga, gb, wa, wb, wo, ln_g, ln_b, alpha):
    g, r, d = x3.shape
    tm = ks * r
    row = lambda arr: pl.BlockSpec((tm, arr.shape[1]), lambda i: (i, 0))
    full = lambda arr: pl.BlockSpec(arr.shape, lambda i: (0,) * arr.ndim)
    return pl.pallas_call(
        functools.partial(_merge_kernel, alpha=alpha),
        grid=(g // ks,),
        in_specs=[pl.BlockSpec((ks, r, d), lambda i: (i, 0, 0)),
                  pl.BlockSpec((N_ADA, ks, 1, d), lambda i: (0, i, 0, 0)),
                  row(a), row(o), row(ga), row(gb), full(wa), full(wb), full(wo), full(ln_g), full(ln_b)],
        out_specs=pl.BlockSpec((ks, r, d), lambda i: (i, 0, 0)),
        out_shape=jax.ShapeDtypeStruct((g, r, d), F32),
        compiler_params=_cparams(("arbitrary",)),
        name="merge_ln1",
    )(x3, mod4, a, o, ga, gb, wa, wb, wo, ln_g, ln_b)


def _top_rows_exact(s, n_top):
    rows = s.shape[0]
    row_id = lax.broadcasted_iota(I32, s.shape, 0)
    rank = jnp.full(s.shape, n_top, I32)
    vals = []
    for r in range(n_top):
        m = jnp.max(s, axis=0, keepdims=True)
        idx = jnp.min(jnp.where(s == m, row_id, rows), axis=0, keepdims=True)
        hit = row_id == idx
        vals.append(m)
        rank = jnp.where(hit, r, rank)
        s = jnp.where(hit, -jnp.inf, s)
    return jnp.concatenate(vals, axis=0), rank


def _top_rows_distinct(s, n_top):
    rank = jnp.full(s.shape, n_top, I32)
    vals = []
    for r in range(n_top):
        m = jnp.max(s, axis=0, keepdims=True)
        hit = s == m
        vals.append(m)
        rank = jnp.where(hit, r, rank)
        s = jnp.where(hit, -jnp.inf, s)
    return jnp.concatenate(vals, axis=0), rank


def _rank_ties(rank, n_top):
    taken = jnp.sum(jnp.where(rank < n_top, 1.0, 0.0), axis=0, keepdims=True)
    return jnp.max(jnp.where(taken != n_top, 1.0, 0.0)) > 0.5


PEER_PAIRS = [(a, b) for a in range(PEER_TOPK) for b in range(PEER_TOPK) if (a + 1) * (b + 1) <= PEER_TOPK]
PEER_CAND_ROWS = -(-len(PEER_PAIRS) // SUBLANES) * SUBLANES


def _peer_kernel(x_ref, mod_ref, wq_ref, kt_ref, u_ref, vt_ref, g_ref, b_ref, y_ref,
                 h2_sc, st_sc, e0_sc, nb_sc, e1_sc, r1_sc, acc_sc, sv_sc, rk_sc, ts_sc, r2_sc, act_sc, w_sc,
                 *, alpha, n_keys):
    ks, r, d = x_ref.shape
    tm = ks * r
    step = pl.program_id(1)
    sub = PEER_EXPERT_TILE // n_keys

    @pl.when(step == 0)
    def _():
        h2 = (x_ref[...] * (1.0 + mod_ref[4]) + mod_ref[3]).reshape(tm, d).astype(BF16)
        h2_sc[...] = h2
        qp = _dot(h2, wq_ref[...]).astype(BF16)
        st_sc[...] = _dot_nt(kt_ref[...], qp)
        acc_sc[...] = jnp.zeros(acc_sc.shape, F32)

        def route(h, carry):
            s0 = st_sc[pl.ds(pl.multiple_of(h * 2 * n_keys, 2 * n_keys), n_keys), :]
            s1 = st_sc[pl.ds(pl.multiple_of(h * 2 * n_keys + n_keys, n_keys), n_keys), :]

            def halves(top):
                sv_sc[0], rk_sc[0] = top(s0, PEER_TOPK)
                sv_sc[1], rk_sc[1] = top(s1, PEER_TOPK)

            halves(_top_rows_distinct)

            @pl.when(jnp.logical_or(_rank_ties(rk_sc[0], PEER_TOPK), _rank_ties(rk_sc[1], PEER_TOPK)))
            def _():
                halves(_top_rows_exact)

            v0, v1, rank0 = sv_sc[0], sv_sc[1], rk_sc[0]
            pad = [jnp.full((PEER_CAND_ROWS - len(PEER_PAIRS), tm), -jnp.inf, F32)]
            cand = jnp.concatenate([v0[a:a + 1] + v1[b:b + 1] for a, b in PEER_PAIRS] + pad, axis=0)

            def pairs(top):
                ts_sc[...], r2_sc[...] = top(cand, PEER_TOPK)

            pairs(_top_rows_distinct)

            @pl.when(_rank_ties(r2_sc[...], PEER_TOPK))
            def _():
                pairs(_top_rows_exact)

            top_s, rank2 = ts_sc[...], r2_sc[...]
            z = jnp.sum(jnp.exp(top_s - top_s[0:1]), axis=0, keepdims=True)
            took = jnp.where(rank2 < PEER_TOPK, 1, 0)
            take = jnp.zeros(rank0.shape, I32)
            for a in range(PEER_TOPK):
                n_a = sum(took[i:i + 1] for i, (pa, _) in enumerate(PEER_PAIRS) if pa == a)
                take = jnp.where(rank0 == a, n_a, take)
            e0_sc[h] = (jnp.exp(s0 - v0[0:1]) / z).reshape(n_keys // SUBLANES, SUBLANES, tm)
            nb_sc[h] = take.reshape(n_keys // SUBLANES, SUBLANES, tm)
            e1_sc[h] = jnp.exp(s1 - v1[0:1])
            r1_sc[h] = rk_sc[1]
            return carry

        lax.fori_loop(0, PEER_HEADS, route, 0)

    n_tok = tm // PEER_TOKEN_SPLIT
    for t in range(PEER_TOKEN_SPLIT):
        act_sc[t] = _dot_nt(u_ref[...], h2_sc[t * n_tok:(t + 1) * n_tok, :])
    for t in range(PEER_TOKEN_SPLIT):
        lanes = slice(t * n_tok, (t + 1) * n_tok)
        for ii in range(sub):
            act = act_sc[t, ii * n_keys:(ii + 1) * n_keys, :]
            gate = jnp.zeros((n_keys, n_tok), F32)
            for h in range(PEER_HEADS):
                take_i = nb_sc[h, step, ii:ii + 1, lanes]
                e0_i = e0_sc[h, step, ii:ii + 1, lanes]
                gate = gate + jnp.where(r1_sc[h, :, lanes] < take_i, e1_sc[h, :, lanes], 0.0) * e0_i
            gelu = 0.5 * act * (1.0 + lax.erf(act * (2.0 ** -0.5)))
            w_sc[t, ii * n_keys:(ii + 1) * n_keys, :] = (gelu * gate).astype(BF16)
    for t in range(PEER_TOKEN_SPLIT):
        acc_sc[:, t * n_tok:(t + 1) * n_tok] += _dot(vt_ref[...], w_sc[t])

    @pl.when(step == pl.num_programs(1) - 1)
    def _():
        f = acc_sc[...].T.reshape(ks, r, d)
        y = alpha * x_ref[...] + mod_ref[5] * f
        y_ref[...] = _layer_norm(y, g_ref[...], b_ref[...])


def _peer(x3, mod4, ks, wq, kt, u_b, vt_b, ln_g, ln_b, alpha, n_keys):
    g, r, d = x3.shape
    tm = ks * r
    n_exp = u_b.shape[0]
    assert PEER_EXPERT_TILE == SUBLANES * n_keys and n_exp % PEER_EXPERT_TILE == 0
    full = lambda arr: pl.BlockSpec(arr.shape, lambda i, e: (0,) * arr.ndim)
    return pl.pallas_call(
        functools.partial(_peer_kernel, alpha=alpha, n_keys=n_keys),
        grid=(g // ks, n_exp // PEER_EXPERT_TILE),
        in_specs=[pl.BlockSpec((ks, r, d), lambda i, e: (i, 0, 0)),
                  pl.BlockSpec((N_ADA, ks, 1, d), lambda i, e: (0, i, 0, 0)),
                  full(wq), full(kt),
                  pl.BlockSpec((PEER_EXPERT_TILE, d), lambda i, e: (e, 0)),
                  pl.BlockSpec((d, PEER_EXPERT_TILE), lambda i, e: (0, e)),
                  full(ln_g), full(ln_b)],
        out_specs=pl.BlockSpec((ks, r, d), lambda i, e: (i, 0, 0)),
        out_shape=jax.ShapeDtypeStruct((g, r, d), F32),
        scratch_shapes=[pltpu.VMEM((tm, d), BF16),
                        pltpu.VMEM((kt.shape[0], tm), F32),
                        pltpu.VMEM((PEER_HEADS, n_keys // SUBLANES, SUBLANES, tm), F32),
                        pltpu.VMEM((PEER_HEADS, n_keys // SUBLANES, SUBLANES, tm), I32),
                        pltpu.VMEM((PEER_HEADS, n_keys, tm), F32),
                        pltpu.VMEM((PEER_HEADS, n_keys, tm), I32),
                        pltpu.VMEM((d, tm), F32),
                        pltpu.VMEM((2, PEER_TOPK, tm), F32),
                        pltpu.VMEM((2, n_keys, tm), I32),
                        pltpu.VMEM((PEER_TOPK, tm), F32),
                        pltpu.VMEM((PEER_CAND_ROWS, tm), I32),
                        pltpu.VMEM((PEER_TOKEN_SPLIT, PEER_EXPERT_TILE, tm // PEER_TOKEN_SPLIT), F32),
                        pltpu.VMEM((PEER_TOKEN_SPLIT, PEER_EXPERT_TILE, tm // PEER_TOKEN_SPLIT), BF16)],
        compiler_params=_cparams(("arbitrary", "arbitrary")),
        name="peer_ln2",
    )(x3, mod4, wq, kt, u_b, vt_b, ln_g, ln_b)


def _layer_weights(w_in, pool_w, w_branch_a, w_branch_b, w_out, peer_wq, peer_sub_keys, peer_u, peer_v,
                   *, pw, aw, iw, idim):
    d = w_in.shape[0]
    o = np.cumsum([0, pw, aw, aw, aw, iw, idim, IDX_HEADS, d, d])
    cut = lambda j: w_in[:, o[j]:o[j + 1]]
    u_w, q_w, k_w, v_w, qi_w, ki_w, wi_w, ga_w, gb_w = (cut(j) for j in range(9))
    wn = jnp.concatenate([u_w, k_w, v_w], axis=1).astype(BF16)
    wg = jnp.concatenate([ga_w, gb_w], axis=1).astype(BF16)
    wi_pad = jnp.pad(wi_w, ((0, 0), (0, 2 * SUBLANES - IDX_HEADS)))
    wt = jnp.concatenate([v_w, q_w, qi_w, wi_pad], axis=1).T.astype(BF16)
    n_keys, half = peer_sub_keys.shape[1], peer_sub_keys.shape[2]
    kt = jnp.einsum("hg,cb,ckd->hckgbd", jnp.eye(PEER_HEADS, dtype=F32), jnp.eye(2, dtype=F32), peer_sub_keys)
    kt = kt.reshape(PEER_HEADS * 2 * n_keys, PEER_HEADS * 2 * half).astype(BF16)
    return dict(wn=wn, wki=ki_w.astype(BF16), wg=wg, wt=wt, pool_w=pool_w.astype(BF16),
                wa=w_branch_a.astype(BF16), wb=w_branch_b.astype(BF16), wo=w_out.astype(BF16),
                wq=peer_wq.astype(BF16), kt=kt, u_b=peer_u.astype(BF16), vt_b=peer_v.T.astype(BF16))


def kernel(x_prompt, x_sample, cache_k, cache_v, cache_kidx, state_pool, page_table, c_prompt, c_sample,
           w_ada, b_ada, w_in, pool_w, pool_scale, w_branch_a, w_branch_b, w_out, ln1_g, ln1_b,
           peer_wq, peer_sub_keys, peer_u, peer_v, ln2_g, ln2_b):
    depth = w_ada.shape[0]
    b, s, d = x_prompt.shape
    db, t_new, _ = x_sample.shape
    page, n_heads, hd = cache_k.shape[2:]
    idim = cache_kidx.shape[3]
    pool_hist, pw = state_pool.shape[2:]
    n_pages = page_table.shape[1]
    past = n_pages * page
    aw, iw = n_heads * hd, IDX_HEADS * idim
    n_keys = peer_sub_keys.shape[2]
    alpha = (2 * depth) ** 0.25
    idx_scale = float(iw) ** -0.5
    q_scale = float(hd) ** -0.5
    assert t_new == SUBLANES and n_heads == SUBLANES and pool_hist < HIST_ROWS and s % TOKEN_TILE == 0
    assert q_scale == 2.0 ** round(math.log2(q_scale))

    n_c = b + db
    c_all = jnp.pad(jnp.concatenate([c_prompt, c_sample], axis=0), ((0, -n_c % SUBLANES), (0, 0)))
    y_p, y_s = x_prompt, x_sample
    k_t = jnp.transpose(cache_k, (0, 1, 3, 4, 2))
    v_t = jnp.transpose(cache_v, (0, 1, 3, 4, 2))
    ki_t = jnp.transpose(cache_kidx, (0, 1, 3, 2))
    outs = [[] for _ in range(8)]
    for l in range(depth):
        w = _layer_weights(w_in[l], pool_w[l], w_branch_a[l], w_branch_b[l], w_out[l], peer_wq[l], peer_sub_keys[l],
                           peer_u[l], peer_v[l], pw=pw, aw=aw, iw=iw, idim=idim)
        mod = _ada(c_all, w_ada[l], b_ada[l])
        mod_p, mod_s = mod[:, :b], mod[:, b:n_c]
        ps, g1, b1, g2, b2 = pool_scale[l][None], ln1_g[l][None], ln1_b[l][None], ln2_g[l][None], ln2_b[l][None]
        proj = functools.partial(_proj, wn=w["wn"], wki=w["wki"], wg=w["wg"], wt=w["wt"],
                                 pw=pw, aw=aw, iw=iw, idim=idim, q_scale=q_scale)

        x3, mod4, ks = _grouped(y_p, mod_p, TOKEN_TILE)
        u, k, v, ki, kb, kib, vt, qt, qit, wit, ga, gb = proj(x3, mod4, ks)
        nck = s // TOKEN_TILE
        a = _pool_prompt(u.reshape(b, s, pw), w["pool_w"], ps)
        o = _attn_prompt(qt.reshape(b, nck, aw, TOKEN_TILE), qit.reshape(b, nck, iw, TOKEN_TILE),
                         wit.reshape(b, nck, 2 * SUBLANES, TOKEN_TILE), kb.reshape(b, nck, TOKEN_TILE, aw),
                         vt.reshape(b, nck, aw, TOKEN_TILE), kib.reshape(b, nck, TOKEN_TILE, idim),
                         n_sel=min(TOPK_MAX, s // 4), idx_scale=idx_scale, n_heads=n_heads, hd=hd, idim=idim)
        x1 = _merge(x3, mod4, ks, a.reshape(b * s, pw), o.reshape(b * s, aw), ga, gb,
                    w["wa"], w["wb"], w["wo"], g1, b1, alpha)
        x3p, mod4p, ksp = _grouped(x1.reshape(b, s, d), mod_p, PEER_TOKEN_TILE)
        y_p = _peer(x3p, mod4p, ksp, w["wq"], w["kt"], w["u_b"], w["vt_b"], g2, b2, alpha, n_keys).reshape(b, s, d)
        outs[0].append(k.reshape(b, s, n_heads, hd))
        outs[1].append(v.reshape(b, s, n_heads, hd))
        outs[2].append(ki.reshape(b, s, idim))
        outs[3].append(u.reshape(b, s, pw)[:, s - pool_hist:])

        x3, mod4, ks = _grouped(y_s, mod_s, TOKEN_TILE)
        u, k, v, ki, kb, kib, vt, qt, qit, wit, ga, gb = proj(x3, mod4, ks)
        hist = state_pool[l]
        hist16 = jnp.pad(hist, ((0, 0), (HIST_ROWS - pool_hist, 0), (0, 0)))
        a = _pool_sample(u.reshape(db, t_new, pw), hist16, w["pool_w"], ps, past)
        untile = lambda z: jnp.moveaxis(z, 0, 1).reshape(z.shape[1], db, t_new)
        group = n_heads // 2
        q_s = jnp.transpose(untile(qt).reshape(2, group, hd, db, t_new), (3, 0, 1, 4, 2))
        qbd = jnp.einsum("bgltd,lm->bgltmd", q_s, jnp.eye(group, dtype=BF16)).reshape(db, 2, group * t_new, group * hd)
        qi2 = jnp.transpose(untile(qit).reshape(IDX_HEADS, idim, db, t_new), (2, 0, 3, 1)).reshape(db, IDX_HEADS * t_new, idim)
        wi3 = jnp.transpose(untile(wit)[:IDX_HEADS], (1, 0, 2))[..., None]
        lane_pad = lambda z: jnp.pad(z, ((0, 0),) * (z.ndim - 1) + ((0, page - t_new),))
        feat_major = lambda z: lane_pad(jnp.transpose(z.reshape(db, t_new, 2, group * hd), (0, 2, 3, 1)))
        knew = feat_major(kb)
        vnew = feat_major(v.astype(BF16))
        kinew = lane_pad(jnp.transpose(kib.reshape(db, t_new, idim), (0, 2, 1)))
        o = _attn_sample(page_table, qbd, qi2, wi3, knew, vnew, kinew, k_t, v_t, ki_t, l,
                         n_sel=min(TOPK_MAX, (past + t_new) // 4), idx_scale=idx_scale)
        o = jnp.transpose(o, (0, 2, 1, 3)).reshape(db * t_new, aw).astype(BF16)
        x1 = _merge(x3, mod4, ks, a, o, ga, gb, w["wa"], w["wb"], w["wo"], g1, b1, alpha)
        x3p, mod4p, ksp = _grouped(x1, mod_s, PEER_TOKEN_TILE)
        y_s = _peer(x3p, mod4p, ksp, w["wq"], w["kt"], w["u_b"], w["vt_b"], g2, b2, alpha, n_keys)
        outs[4].append(k.reshape(db, t_new, n_heads, hd))
        outs[5].append(v.reshape(db, t_new, n_heads, hd))
        outs[6].append(ki.reshape(db, t_new, idim))
        outs[7].append(jnp.concatenate([hist, u.reshape(db, t_new, pw)], axis=1)[:, t_new:])
    return (y_p, y_s) + tuple(jnp.stack(o_) for o_ in outs)
```

```python
import functools
import math

import jax
import jax.numpy as jnp
import numpy as np
from jax import lax
from jax.experimental import pallas as pl
from jax.experimental.pallas import tpu as pltpu

F32, BF16, I32 = jnp.float32, jnp.bfloat16, jnp.int32

POOL_WINDOWS = (2, 4, 8, 16)
IDX_HEADS = 8
TOPK_MAX = 256
PEER_HEADS = 8
PEER_TOPK = 16
LN_EPS = 1e-5
N_ADA = 6

VMEM_LIMIT_BYTES = 56 * 1024 * 1024
SUBLANES = 8
LANES = 128
BF16_ROWS = 16
TOKEN_TILE = 256
PEER_TOKEN_TILE = 512
PEER_EXPERT_TILE = 1024
PEER_TOKEN_SPLIT = 2
SAMPLE_SEARCH_BITS = 4
HIST_ROWS = 16

INT_MIN = -(2 ** 31)
KEY_NEG_INF = -2139095041
NEG_BIG = -1e30


def _cparams(sem):
    return pltpu.CompilerParams(dimension_semantics=sem, vmem_limit_bytes=VMEM_LIMIT_BYTES)


def _dot(a, b):
    return jnp.dot(a, b, preferred_element_type=F32)


def _dot_nt(a, b):
    return lax.dot_general(a, b, (((1,), (1,)), ((), ())), preferred_element_type=F32)


def _key_to_float(key):
    return lax.bitcast_convert_type(key ^ ((key >> 31) & 0x7FFFFFFF), F32)


def _threshold_step(i, key, count_ge, n_sel, bits=1):
    unit = lax.shift_left(jnp.int32(1), 32 - bits * (i + 1))
    digit = jnp.zeros_like(key)
    for j in range(1, 2 ** bits):
        trial = key + j * unit
        ok = jnp.logical_or(count_ge(_key_to_float(trial)) >= n_sel, trial < KEY_NEG_INF)
        digit = digit + jnp.where(ok, 1, 0)
    return key + digit * unit


def _ada_kernel(c_ref, w_ref, b_ref, o_ref):
    o_ref[...] = _dot(c_ref[...].astype(BF16), w_ref[...].astype(BF16)) + b_ref[...]


def _ada(c_all, w_ada, b_ada):
    n, d = c_all.shape
    return pl.pallas_call(
        _ada_kernel,
        grid=(N_ADA,),
        in_specs=[pl.BlockSpec((n, d), lambda j: (0, 0)),
                  pl.BlockSpec((d, d), lambda j: (0, j)),
                  pl.BlockSpec((1, d), lambda j: (0, j))],
        out_specs=pl.BlockSpec((None, n, d), lambda j: (j, 0, 0)),
        out_shape=jax.ShapeDtypeStruct((N_ADA, n, d), F32),
        compiler_params=_cparams(("arbitrary",)),
        name="ada",
    )(c_all, w_ada, b_ada.reshape(1, -1))


def _grouped(x, mod, rows):
    nseq, l, d = x.shape
    if l >= rows:
        assert l % rows == 0
        per = l // rows
        x3 = x.reshape(nseq * per, rows, d)
        mod4 = jnp.broadcast_to(mod[:, :, None, None, :], (N_ADA, nseq, per, 1, d)).reshape(N_ADA, nseq * per, 1, d)
        return x3, mod4, 1
    assert rows % l == 0 and nseq % (rows // l) == 0
    return x, mod[:, :, None, :], rows // l


def _proj_kernel(x_ref, mod_ref, wn_ref, wki_ref, wg_ref, wt_ref,
                 u_ref, k_ref, v_ref, ki_ref, kb_ref, kib_ref, vt_ref, qt_ref, qit_ref, wit_ref, ga_ref, gb_ref,
                 *, pw, aw, iw, q_scale):
    ks, r, d = x_ref.shape
    h = (x_ref[...] * (1.0 + mod_ref[1]) + mod_ref[0]).reshape(ks * r, d).astype(BF16)
    u_ref[...] = _dot(h, wn_ref[:, 0:pw])
    k = _dot(h, wn_ref[:, pw:pw + aw])
    k_ref[...] = k
    kb_ref[...] = k.astype(BF16)
    v_ref[...] = _dot(h, wn_ref[:, pw + aw:pw + 2 * aw])
    ki = _dot(h, wki_ref[...])
    ki_ref[...] = ki
    kib_ref[...] = ki.astype(BF16)
    ga_ref[...] = _dot(h, wg_ref[:, 0:d])
    gb_ref[...] = _dot(h, wg_ref[:, d:2 * d])
    vt_ref[...] = _dot_nt(wt_ref[0:aw, :], h).astype(BF16)
    qt_ref[...] = (_dot_nt(wt_ref[aw:2 * aw, :], h) * q_scale).astype(BF16)
    qit_ref[...] = _dot_nt(wt_ref[2 * aw:2 * aw + iw, :], h).astype(BF16)
    wit_ref[...] = _dot_nt(wt_ref[2 * aw + iw:2 * aw + iw + 2 * SUBLANES, :], h)


def _proj(x3, mod4, ks, wn, wki, wg, wt, *, pw, aw, iw, idim, q_scale):
    g, r, d = x3.shape
    tm = ks * r
    nb = g // ks
    n = g * r
    row = lambda cols: pl.BlockSpec((tm, cols), lambda i: (i, 0))
    col = lambda rows: pl.BlockSpec((None, rows, tm), lambda i: (i, 0, 0))
    full = lambda a: pl.BlockSpec(a.shape, lambda i: (0,) * a.ndim)
    outs = [((n, pw), F32, row(pw)), ((n, aw), F32, row(aw)), ((n, aw), F32, row(aw)), ((n, idim), F32, row(idim)),
            ((n, aw), BF16, row(aw)), ((n, idim), BF16, row(idim)),
            ((nb, aw, tm), BF16, col(aw)), ((nb, aw, tm), BF16, col(aw)), ((nb, iw, tm), BF16, col(iw)),
            ((nb, 2 * SUBLANES, tm), F32, col(2 * SUBLANES)),
            ((n, d), F32, row(d)), ((n, d), F32, row(d))]
    return pl.pallas_call(
        functools.partial(_proj_kernel, pw=pw, aw=aw, iw=iw, q_scale=q_scale),
        grid=(nb,),
        in_specs=[pl.BlockSpec((ks, r, d), lambda i: (i, 0, 0)),
                  pl.BlockSpec((N_ADA, ks, 1, d), lambda i: (0, i, 0, 0)),
                  full(wn), full(wki), full(wg), full(wt)],
        out_specs=[o[2] for o in outs],
        out_shape=[jax.ShapeDtypeStruct(o[0], o[1]) for o in outs],
        compiler_params=_cparams(("arbitrary",)),
        name="in_proj",
    )(x3, mod4, wn, wki, wg, wt)


def _pool_windows(ext_ref, pos, pw_ref, ps_ref, a_ref):
    ks, rows, width = ext_ref.shape
    r = rows - HIST_ROWS
    gw = width // len(POOL_WINDOWS)
    for g, w in enumerate(POOL_WINDOWS):
        lo = g * gw
        cur = ext_ref[:, HIST_ROWS:HIST_ROWS + r, lo:lo + gw]
        acc = cur
        for j in range(1, w):
            acc = acc + ext_ref[:, HIST_ROWS - j:HIST_ROWS - j + r, lo:lo + gw]
        cnt = jnp.minimum(pos + 1, w).astype(F32)
        pooled = (acc / cnt - cur).reshape(ks * r, gw).astype(BF16)
        mixed = _dot(pooled, pw_ref[g])
        a_ref[:, lo:lo + gw] = (mixed * ps_ref[:, lo:lo + gw]).astype(BF16)


def _pool_prompt_kernel(u_ref, pw_ref, ps_ref, a_ref, ext_ref):
    i = pl.program_id(1)
    tp, width = u_ref.shape
    gw = width // len(POOL_WINDOWS)

    @pl.when(i == 0)
    def _():
        ext_ref[:, 0:HIST_ROWS, :] = jnp.zeros((1, HIST_ROWS, width), F32)

    @pl.when(i > 0)
    def _():
        ext_ref[:, 0:HIST_ROWS, :] = ext_ref[:, tp:tp + HIST_ROWS, :]

    ext_ref[:, HIST_ROWS:, :] = u_ref[...][None]
    pos = i * tp + lax.broadcasted_iota(I32, (1, tp, gw), 1)
    _pool_windows(ext_ref, pos, pw_ref, ps_ref, a_ref)


def _pool_prompt(u, pool_w, pool_scale):
    b, s, width = u.shape
    tp = TOKEN_TILE
    return pl.pallas_call(
        _pool_prompt_kernel,
        grid=(b, s // tp),
        in_specs=[pl.BlockSpec((None, tp, width), lambda bi, i: (bi, i, 0)),
                  pl.BlockSpec(pool_w.shape, lambda bi, i: (0, 0, 0)),
                  pl.BlockSpec(pool_scale.shape, lambda bi, i: (0, 0))],
        out_specs=pl.BlockSpec((None, tp, width), lambda bi, i: (bi, i, 0)),
        out_shape=jax.ShapeDtypeStruct((b, s, width), BF16),
        scratch_shapes=[pltpu.VMEM((1, HIST_ROWS + tp, width), F32)],
        compiler_params=_cparams(("arbitrary", "arbitrary")),
        name="pool_prompt",
    )(u, pool_w, pool_scale)


def _pool_sample_kernel(u_ref, hist_ref, pw_ref, ps_ref, a_ref, ext_ref, *, pos0):
    ks, t, width = u_ref.shape
    gw = width // len(POOL_WINDOWS)
    ext_ref[:, 0:HIST_ROWS, :] = hist_ref[...]
    ext_ref[:, HIST_ROWS:, :] = u_ref[...]
    pos = pos0 + lax.broadcasted_iota(I32, (1, t, gw), 1)
    _pool_windows(ext_ref, pos, pw_ref, ps_ref, a_ref)


def _pool_sample(u, hist16, pool_w, pool_scale, pos0):
    db, t, width = u.shape
    ks = TOKEN_TILE // t
    return pl.pallas_call(
        functools.partial(_pool_sample_kernel, pos0=pos0),
        grid=(db // ks,),
        in_specs=[pl.BlockSpec((ks, t, width), lambda i: (i, 0, 0)),
                  pl.BlockSpec((ks, HIST_ROWS, width), lambda i: (i, 0, 0)),
                  pl.BlockSpec(pool_w.shape, lambda i: (0, 0, 0)),
                  pl.BlockSpec(pool_scale.shape, lambda i: (0, 0))],
        out_specs=pl.BlockSpec((ks * t, width), lambda i: (i, 0)),
        out_shape=jax.ShapeDtypeStruct((db * t, width), BF16),
        scratch_shapes=[pltpu.VMEM((ks, HIST_ROWS + t, width), F32)],
        compiler_params=_cparams(("arbitrary",)),
        name="pool_sample",
    )(u, hist16, pool_w, pool_scale)


def _alibi_slopes(n_heads):
    return [2.0 ** (-8.0 * (h + 1) / n_heads) for h in range(n_heads)]


def _attn_prompt_kernel(qt_ref, qit_ref, wit_ref, kb_ref, vt_ref, kib_ref, o_ref,
                        sc_sc, qz_sc, acc_sc, lg_sc, *, n_sel, idx_scale, n_heads, hd, idim, seq_len):
    qb = pl.program_id(1)
    tq = qt_ref.shape[-1]
    sk = tq
    nchunk = qb + 1
    t_idx = qb * tq + lax.broadcasted_iota(I32, (sk, tq), 1)
    s_loc = lax.broadcasted_iota(I32, (sk, tq), 0)

    def score_body(c, carry):
        kic = kib_ref[c]
        acc = jnp.zeros((sk, tq), F32)
        for h in range(IDX_HEADS):
            rel = _dot(kic, qit_ref[h * idim:(h + 1) * idim, :])
            acc = acc + wit_ref[h:h + 1, :] * jnp.maximum(rel, 0.0)
        sc_sc[c] = jnp.where(c * sk + s_loc <= t_idx, acc * idx_scale, -jnp.inf)
        return carry

    lax.fori_loop(0, nchunk, score_body, 0)

    def colsum(mask):
        return jnp.where(mask, 1.0, 0.0).reshape(sk // SUBLANES, SUBLANES, tq).sum(axis=0)

    @pl.when(nchunk % 2 == 1)
    def _():
        sc_sc[nchunk] = jnp.full((sk, tq), -jnp.inf, F32)

    def count(pred):
        def pair(c2, a):
            c = 2 * c2
            return a + colsum(pred(c, sc_sc[c])) + colsum(pred(c + 1, sc_sc[c + 1]))
        acc = lax.fori_loop(0, (nchunk + 1) // 2, pair, jnp.zeros((SUBLANES, tq), F32))
        return acc.sum(axis=0, keepdims=True)

    count_ge = lambda t: count(lambda c, s: s >= t)
    tau_key = lax.fori_loop(0, 32, lambda i, k: _threshold_step(i, k, count_ge, n_sel),
                            jnp.full((1, tq), INT_MIN, I32))
    tau = _key_to_float(tau_key)

    tie_rows = jnp.logical_and(count_ge(tau) > n_sel, tau_key > KEY_NEG_INF)
    has_tie = jnp.max(jnp.where(tie_rows, 1.0, 0.0)) > 0.5

    @pl.when(has_tie)
    def _():
        need = n_sel - count(lambda c, s: s > tau)
        nbits = max(1, int(math.ceil(math.log2(seq_len))))

        def first_enough(i, j):
            step = lax.shift_left(jnp.int32(1), nbits - 1 - i)
            upto = count(lambda c, s: jnp.logical_and(s == tau, c * sk + s_loc <= j + step - 1))
            return jnp.where(upto < need, j + step, j)

        last = lax.fori_loop(0, nbits, first_enough, jnp.zeros((1, tq), I32))

        def demote(c, carry):
            s = sc_sc[c]
            drop = jnp.logical_and(jnp.logical_and(tie_rows, s == tau), c * sk + s_loc > last)
            sc_sc[c] = jnp.where(drop, -jnp.inf, s)
            return carry

        lax.fori_loop(0, nchunk, demote, 0)

    pair_rows = lax.broadcasted_iota(I32, (2 * hd, tq), 0)
    for h in range(n_heads):
        blk = qt_ref[(h // 2) * 2 * hd:(h // 2 + 1) * 2 * hd, :]
        keep = pair_rows < hd if h % 2 == 0 else pair_rows >= hd
        qz_sc[h] = jnp.where(keep, blk, jnp.zeros_like(blk))
    acc_sc[...] = jnp.zeros(acc_sc.shape, F32)
    slopes = _alibi_slopes(n_heads)

    def att_body(c, carry):
        m_all, l_all = carry
        pos = c * sk + s_loc
        sel = jnp.logical_and(sc_sc[c] >= tau, pos <= t_idx)
        dist = (pos - qb * tq).astype(F32)
        for h in range(n_heads):
            lg_sc[h] = _dot(kb_ref[c, :, (h // 2) * 2 * hd:(h // 2 + 1) * 2 * hd], qz_sc[h])
        m_out, l_out = [], []
        for h in range(n_heads):
            lg = jnp.where(sel, lg_sc[h] + slopes[h] * dist, -jnp.inf)
            m_new = jnp.maximum(m_all[h], jnp.max(lg, axis=0, keepdims=True))
            alpha = jnp.exp(m_all[h] - m_new)
            p = jnp.exp(lg - m_new)
            l_out.append(alpha * l_all[h] + jnp.sum(p, axis=0, keepdims=True))
            m_out.append(m_new)
            pv = _dot(vt_ref[c, h * hd:(h + 1) * hd, :], p.astype(BF16))
            acc_sc[h * hd:(h + 1) * hd, :] = alpha * acc_sc[h * hd:(h + 1) * hd, :] + pv
        return tuple(m_out), tuple(l_out)

    init = (tuple(jnp.full((1, tq), NEG_BIG, F32) for _ in range(n_heads)),
            tuple(jnp.zeros((1, tq), F32) for _ in range(n_heads)))
    _, l_all = lax.fori_loop(0, nchunk, att_body, init)
    out_t = jnp.concatenate([acc_sc[h * hd:(h + 1) * hd, :] / l_all[h] for h in range(n_heads)], axis=0)
    o_ref[...] = out_t.T.astype(BF16)


def _attn_prompt(qt, qit, wit, kb, vt, kib, *, n_sel, idx_scale, n_heads, hd, idim):
    b, nck, aw, tq = qt.shape
    iw = qit.shape[2]
    resident = lambda shape: pl.BlockSpec((None,) + shape, lambda bi, i: (bi, 0, 0, 0), pipeline_mode=pl.Buffered(1))
    tile = lambda rows: pl.BlockSpec((None, None, rows, tq), lambda bi, i: (bi, i, 0, 0))
    return pl.pallas_call(
        functools.partial(_attn_prompt_kernel, n_sel=n_sel, idx_scale=idx_scale, n_heads=n_heads, hd=hd, idim=idim,
                          seq_len=nck * tq),
        grid=(b, nck),
        in_specs=[tile(aw), tile(iw), tile(2 * SUBLANES),
                  resident((nck, tq, aw)), resident((nck, aw, tq)), resident((nck, tq, idim))],
        out_specs=pl.BlockSpec((None, tq, aw), lambda bi, i: (bi, i, 0)),
        out_shape=jax.ShapeDtypeStruct((b, nck * tq, aw), BF16),
        scratch_shapes=[pltpu.VMEM((nck + nck % 2, tq, tq), F32),
                        pltpu.VMEM((n_heads, 2 * hd, tq), BF16),
                        pltpu.VMEM((aw, tq), F32),
                        pltpu.VMEM((n_heads, tq, tq), F32)],
        compiler_params=_cparams(("arbitrary", "arbitrary")),
        name="attn_prompt",
    )(qt, qit, wit, kb, vt, kib)


def _attn_sample_kernel(pt_ref, qbd_ref, qi_ref, wi_ref, knew_ref, vnew_ref, kinew_ref, *rest,
                        n_pages, page, n_sel, idx_scale, n_heads, hd, t_new):
    del pt_ref
    ki_refs, k_refs, v_refs = rest[0:n_pages], rest[n_pages:2 * n_pages], rest[2 * n_pages:3 * n_pages]
    o_ref, sc_sc, lg_sc = rest[3 * n_pages:]
    past = n_pages * page
    width = past + page
    group = n_heads // 2
    row = lax.broadcasted_iota(I32, (t_new, page), 0)
    lane = lax.broadcasted_iota(I32, (t_new, page), 1)
    wi = wi_ref[...]

    def scores(kit):
        rel = jnp.maximum(_dot(qi_ref[...], kit), 0.0)
        return jnp.sum(rel.reshape(IDX_HEADS, t_new, rel.shape[1]) * wi, axis=0) * idx_scale

    def page_pair(refs, pp, head_lo=None):
        if head_lo is None:
            parts = [refs[2 * pp + j][...] for j in range(2)]
        else:
            parts = [refs[2 * pp + j][head_lo:head_lo + group].reshape(group * hd, page) for j in range(2)]
        return jnp.concatenate(parts, axis=1).astype(BF16)

    for pp in range(n_pages // 2):
        sc_sc[:, 2 * pp * page:(2 * pp + 2) * page] = scores(page_pair(ki_refs, pp))
    sc_sc[:, past:width] = jnp.where(lane <= row, scores(kinew_ref[...]), -jnp.inf)

    def count(pred):
        return jnp.sum(jnp.where(pred(sc_sc[...]), 1.0, 0.0), axis=1, keepdims=True)

    count_ge = lambda t: count(lambda s: s >= t)
    tau_key = lax.fori_loop(0, 32 // SAMPLE_SEARCH_BITS,
                            lambda i, k: _threshold_step(i, k, count_ge, n_sel, SAMPLE_SEARCH_BITS),
                            jnp.full((t_new, 1), INT_MIN, I32))
    tau = _key_to_float(tau_key)

    tie_rows = jnp.logical_and(count_ge(tau) > n_sel, tau_key > KEY_NEG_INF)
    has_tie = jnp.max(jnp.where(tie_rows, 1.0, 0.0)) > 0.5
    pos = lax.broadcasted_iota(I32, (t_new, width), 1)

    @pl.when(has_tie)
    def _():
        need = n_sel - count(lambda s: s > tau)
        nbits = max(1, int(math.ceil(math.log2(width))))

        def first_enough(i, j):
            step = lax.shift_left(jnp.int32(1), nbits - 1 - i)
            upto = count(lambda s: jnp.logical_and(s == tau, pos <= j + step - 1))
            return jnp.where(upto < need, j + step, j)

        last = lax.fori_loop(0, nbits, first_enough, jnp.zeros((t_new, 1), I32))
        s = sc_sc[...]
        drop = jnp.logical_and(jnp.logical_and(tie_rows, s == tau), pos > last)
        sc_sc[...] = jnp.where(drop, -jnp.inf, s)

    slopes = _alibi_slopes(n_heads)
    t_pos = past + lax.broadcasted_iota(I32, (t_new, width), 0)
    sel = jnp.logical_and(sc_sc[...] >= tau, pos <= t_pos)
    bias = jnp.where(sel, 0.0, -jnp.inf)
    dist = (t_pos - pos).astype(F32)
    spans = [(2 * pp * page, 2 * page) for pp in range(n_pages // 2)] + [(past, page)]
    for g in range(2):
        for pp, (lo, n) in enumerate(spans):
            kt = page_pair(k_refs, pp, g * group) if pp < n_pages // 2 else knew_ref[g]
            lg = _dot(qbd_ref[g], kt).reshape(group, t_new, n)
            for hl in range(group):
                h = g * group + hl
                lg_sc[h, :, lo:lo + n] = lg[hl] - slopes[h] * dist[:, lo:lo + n] + bias[:, lo:lo + n]
    for g in range(2):
        probs, denoms = [], []
        for hl in range(group):
            lg = lg_sc[g * group + hl]
            pr = jnp.exp(lg - jnp.max(lg, axis=1, keepdims=True))
            denoms.append(jnp.sum(pr, axis=1, keepdims=True))
            probs.append(pr)
        pg = jnp.concatenate(probs, axis=0).astype(BF16)
        acc = jnp.zeros((group * t_new, group * hd), F32)
        for pp, (lo, n) in enumerate(spans):
            vt = page_pair(v_refs, pp, g * group) if pp < n_pages // 2 else vnew_ref[g]
            acc = acc + _dot_nt(pg[:, lo:lo + n], vt)
        for hl in range(group):
            o_ref[g * group + hl] = acc[hl * t_new:(hl + 1) * t_new, hl * hd:(hl + 1) * hd] / denoms[hl]


def _attn_sample(page_table, qbd, qi2, wi3, knew, vnew, kinew, k_t, v_t, ki_t, layer, *, n_sel, idx_scale):
    db = qbd.shape[0]
    n_pages = page_table.shape[1]
    n_heads, hd, page = k_t.shape[2:]
    idim = ki_t.shape[2]
    t_new = wi3.shape[2]
    assert n_pages % 2 == 0 and n_heads % 2 == 0
    per_seq = lambda a: pl.BlockSpec((None,) + a.shape[1:], lambda b, pt: (b,) + (0,) * (a.ndim - 1))
    kv_page = lambda j: pl.BlockSpec((None, None, n_heads, hd, page), lambda b, pt: (layer, pt[b, j], 0, 0, 0))
    ki_page = lambda j: pl.BlockSpec((None, None, idim, page), lambda b, pt: (layer, pt[b, j], 0, 0))
    in_specs = ([per_seq(qbd), per_seq(qi2), per_seq(wi3), per_seq(knew), per_seq(vnew), per_seq(kinew)]
                + [ki_page(j) for j in range(n_pages)]
                + [kv_page(j) for j in range(n_pages)]
                + [kv_page(j) for j in range(n_pages)])
    grid_spec = pltpu.PrefetchScalarGridSpec(
        num_scalar_prefetch=1,
        grid=(db,),
        in_specs=in_specs,
        out_specs=pl.BlockSpec((None, n_heads, t_new, hd), lambda b, pt: (b, 0, 0, 0)),
        scratch_shapes=[pltpu.VMEM((t_new, (n_pages + 1) * page), F32),
                        pltpu.VMEM((n_heads, t_new, (n_pages + 1) * page), F32)],
    )
    return pl.pallas_call(
        functools.partial(_attn_sample_kernel, n_pages=n_pages, page=page, n_sel=n_sel, idx_scale=idx_scale,
                          n_heads=n_heads, hd=hd, t_new=t_new),
        grid_spec=grid_spec,
        out_shape=jax.ShapeDtypeStruct((db, n_heads, t_new, hd), F32),
        compiler_params=_cparams(("arbitrary",)),
        name="attn_sample",
    )(page_table, qbd, qi2, wi3, knew, vnew, kinew, *([ki_t] * n_pages), *([k_t] * n_pages), *([v_t] * n_pages))


def _layer_norm(y, g, b):
    mu = jnp.mean(y, axis=-1, keepdims=True)
    var = jnp.mean(jnp.square(y - mu), axis=-1, keepdims=True)
    return (y - mu) * lax.rsqrt(var + LN_EPS) * g + b


def _merge_kernel(x_ref, mod_ref, a_ref, o_ref, ga_ref, gb_ref, wa_ref, wb_ref, wo_ref, g_ref, b_ref, x1_ref, *, alpha):
    ks, r, d = x_ref.shape
    merged = (jax.nn.sigmoid(ga_ref[...]) * _dot(a_ref[...], wa_ref[...])
              + jax.nn.sigmoid(gb_ref[...]) * _dot(o_ref[...], wb_ref[...]))
    upd = _dot(merged.astype(BF16), wo_ref[...]).reshape(ks, r, d)
    y = alpha * x_ref[...] + mod_ref[2] * upd
    x1_ref[...] = _layer_norm(y, g_ref[...], b_ref[...])


def _merge(x3, mod4, ks, a, o, ga, gb, wa, wb, wo, ln_g, ln_b, alpha):
    g, r, d = x3.shape
    tm = ks * r
    row = lambda arr: pl.BlockSpec((tm, arr.shape[1]), lambda i: (i, 0))
    full = lambda arr: pl.BlockSpec(arr.shape, lambda i: (0,) * arr.ndim)
    return pl.pallas_call(
        functools.partial(_merge_kernel, alpha=alpha),
        grid=(g // ks,),
        in_specs=[pl.BlockSpec((ks, r, d), lambda i: (i, 0, 0)),
                  pl.BlockSpec((N_ADA, ks, 1, d), lambda i: (0, i, 0, 0)),
                  row(a), row(o), row(ga), row(gb), full(wa), full(wb), full(wo), full(ln_g), full(ln_b)],
        out_specs=pl.BlockSpec((ks, r, d), lambda i: (i, 0, 0)),
        out_shape=jax.ShapeDtypeStruct((g, r, d), F32),
        compiler_params=_cparams(("arbitrary",)),
        name="merge_ln1",
    )(x3, mod4, a, o, ga, gb, wa, wb, wo, ln_g, ln_b)


def _top_rows_exact(s, n_top):
    rows = s.shape[0]
    row_id = lax.broadcasted_iota(I32, s.shape, 0)
    rank = jnp.full(s.shape, n_top, I32)
    vals = []
    for r in range(n_top):
        m = jnp.max(s, axis=0, keepdims=True)
        idx = jnp.min(jnp.where(s == m, row_id, rows), axis=0, keepdims=True)
        hit = row_id == idx
        vals.append(m)
        rank = jnp.where(hit, r, rank)
        s = jnp.where(hit, -jnp.inf, s)
    return jnp.concatenate(vals, axis=0), rank


def _top_rows_distinct(s, n_top):
    rank = jnp.full(s.shape, n_top, I32)
    vals = []
    for r in range(n_top):
        m = jnp.max(s, axis=0, keepdims=True)
        hit = s == m
        vals.append(m)
        rank = jnp.where(hit, r, rank)
        s = jnp.where(hit, -jnp.inf, s)
    return jnp.concatenate(vals, axis=0), rank


def _rank_ties(rank, n_top):
    taken = jnp.sum(jnp.where(rank < n_top, 1.0, 0.0), axis=0, keepdims=True)
    return jnp.max(jnp.where(taken != n_top, 1.0, 0.0)) > 0.5


PEER_PAIRS = [(a, b) for a in range(PEER_TOPK) for b in range(PEER_TOPK) if (a + 1) * (b + 1) <= PEER_TOPK]
PEER_CAND_ROWS = -(-len(PEER_PAIRS) // SUBLANES) * SUBLANES


def _peer_kernel(x_ref, mod_ref, wq_ref, kt_ref, u_ref, vt_ref, g_ref, b_ref, y_ref,
                 h2_sc, st_sc, e0_sc, nb_sc, e1_sc, r1_sc, acc_sc, sv_sc, rk_sc, ts_sc, r2_sc, act_sc, w_sc,
                 *, alpha, n_keys):
    ks, r, d = x_ref.shape
    tm = ks * r
    step = pl.program_id(1)
    sub = PEER_EXPERT_TILE // n_keys

    @pl.when(step == 0)
    def _():
        h2 = (x_ref[...] * (1.0 + mod_ref[4]) + mod_ref[3]).reshape(tm, d).astype(BF16)
        h2_sc[...] = h2
        qp = _dot(h2, wq_ref[...]).astype(BF16)
        st_sc[...] = _dot_nt(kt_ref[...], qp)
        acc_sc[...] = jnp.zeros(acc_sc.shape, F32)

        def route(h, carry):
            s0 = st_sc[pl.ds(pl.multiple_of(h * 2 * n_keys, 2 * n_keys), n_keys), :]
            s1 = st_sc[pl.ds(pl.multiple_of(h * 2 * n_keys + n_keys, n_keys), n_keys), :]

            def halves(top):
                sv_sc[0], rk_sc[0] = top(s0, PEER_TOPK)
                sv_sc[1], rk_sc[1] = top(s1, PEER_TOPK)

            halves(_top_rows_distinct)

            @pl.when(jnp.logical_or(_rank_ties(rk_sc[0], PEER_TOPK), _rank_ties(rk_sc[1], PEER_TOPK)))
            def _():
                halves(_top_rows_exact)

            v0, v1, rank0 = sv_sc[0], sv_sc[1], rk_sc[0]
            pad = [jnp.full((PEER_CAND_ROWS - len(PEER_PAIRS), tm), -jnp.inf, F32)]
            cand = jnp.concatenate([v0[a:a + 1] + v1[b:b + 1] for a, b in PEER_PAIRS] + pad, axis=0)

            def pairs(top):
                ts_sc[...], r2_sc[...] = top(cand, PEER_TOPK)

            pairs(_top_rows_distinct)

            @pl.when(_rank_ties(r2_sc[...], PEER_TOPK))
            def _():
                pairs(_top_rows_exact)

            top_s, rank2 = ts_sc[...], r2_sc[...]
            z = jnp.sum(jnp.exp(top_s - top_s[0:1]), axis=0, keepdims=True)
            took = jnp.where(rank2 < PEER_TOPK, 1, 0)
            take = jnp.zeros(rank0.shape, I32)
            for a in range(PEER_TOPK):
                n_a = sum(took[i:i + 1] for i, (pa, _) in enumerate(PEER_PAIRS) if pa == a)
                take = jnp.where(rank0 == a, n_a, take)
            e0_sc[h] = (jnp.exp(s0 - v0[0:1]) / z).reshape(n_keys // SUBLANES, SUBLANES, tm)
            nb_sc[h] = take.astype(F32).reshape(n_keys // SUBLANES, SUBLANES, tm)
            e1_sc[h] = jnp.exp(s1 - v1[0:1]).astype(BF16).reshape(n_keys // BF16_ROWS, BF16_ROWS, tm)
            r1_sc[h] = rk_sc[1].astype(F32).astype(BF16).reshape(n_keys // BF16_ROWS, BF16_ROWS, tm)
            return carry

        lax.fori_loop(0, PEER_HEADS, route, 0)

    n_tok = tm // PEER_TOKEN_SPLIT
    for t in range(PEER_TOKEN_SPLIT):
        act_sc[t] = _dot_nt(u_ref[...], h2_sc[t * n_tok:(t + 1) * n_tok, :])
    for t in range(PEER_TOKEN_SPLIT):
        lanes = slice(t * n_tok, (t + 1) * n_tok)
        for ii in range(sub):
            act = act_sc[t, ii * n_keys:(ii + 1) * n_keys, :]
            gate = jnp.zeros((n_keys // BF16_ROWS, BF16_ROWS, n_tok), BF16)
            for h in range(PEER_HEADS):
                take_i = jnp.broadcast_to(nb_sc[h, step, ii:ii + 1, lanes], (BF16_ROWS, n_tok)).astype(BF16)
                e0_i = jnp.broadcast_to(e0_sc[h, step, ii:ii + 1, lanes], (BF16_ROWS, n_tok)).astype(BF16)
                e1 = e1_sc[h, :, :, lanes]
                gate = gate + jnp.where(r1_sc[h, :, :, lanes] < take_i, e1, jnp.zeros_like(e1)) * e0_i
            gelu = 0.5 * act * (1.0 + lax.erf(act * (2.0 ** -0.5)))
            w = gelu.astype(BF16).reshape(n_keys // BF16_ROWS, BF16_ROWS, n_tok) * gate
            w_sc[t, ii * n_keys:(ii + 1) * n_keys, :] = w.reshape(n_keys, n_tok)
    for t in range(PEER_TOKEN_SPLIT):
        acc_sc[:, t * n_tok:(t + 1) * n_tok] += _dot(vt_ref[...], w_sc[t])

    @pl.when(step == pl.num_programs(1) - 1)
    def _():
        f = acc_sc[...].T.reshape(ks, r, d)
        y = alpha * x_ref[...] + mod_ref[5] * f
        y_ref[...] = _layer_norm(y, g_ref[...], b_ref[...])


def _peer(x3, mod4, ks, wq, kt, u_b, vt_b, ln_g, ln_b, alpha, n_keys):
    g, r, d = x3.shape
    tm = ks * r
    n_exp = u_b.shape[0]
    assert PEER_EXPERT_TILE == SUBLANES * n_keys and n_exp % PEER_EXPERT_TILE == 0
    full = lambda arr: pl.BlockSpec(arr.shape, lambda i, e: (0,) * arr.ndim)
    return pl.pallas_call(
        functools.partial(_peer_kernel, alpha=alpha, n_keys=n_keys),
        grid=(g // ks, n_exp // PEER_EXPERT_TILE),
        in_specs=[pl.BlockSpec((ks, r, d), lambda i, e: (i, 0, 0)),
                  pl.BlockSpec((N_ADA, ks, 1, d), lambda i, e: (0, i, 0, 0)),
                  full(wq), full(kt),
                  pl.BlockSpec((PEER_EXPERT_TILE, d), lambda i, e: (e, 0)),
                  pl.BlockSpec((d, PEER_EXPERT_TILE), lambda i, e: (0, e)),
                  full(ln_g), full(ln_b)],
        out_specs=pl.BlockSpec((ks, r, d), lambda i, e: (i, 0, 0)),
        out_shape=jax.ShapeDtypeStruct((g, r, d), F32),
        scratch_shapes=[pltpu.VMEM((tm, d), BF16),
                        pltpu.VMEM((kt.shape[0], tm), F32),
                        pltpu.VMEM((PEER_HEADS, n_keys // SUBLANES, SUBLANES, tm), F32),
                        pltpu.VMEM((PEER_HEADS, n_keys // SUBLANES, SUBLANES, tm), F32),
                        pltpu.VMEM((PEER_HEADS, n_keys // BF16_ROWS, BF16_ROWS, tm), BF16),
                        pltpu.VMEM((PEER_HEADS, n_keys // BF16_ROWS, BF16_ROWS, tm), BF16),
                        pltpu.VMEM((d, tm), F32),
                        pltpu.VMEM((2, PEER_TOPK, tm), F32),
                        pltpu.VMEM((2, n_keys, tm), I32),
                        pltpu.VMEM((PEER_TOPK, tm), F32),
                        pltpu.VMEM((PEER_CAND_ROWS, tm), I32),
                        pltpu.VMEM((PEER_TOKEN_SPLIT, PEER_EXPERT_TILE, tm // PEER_TOKEN_SPLIT), F32),
                        pltpu.VMEM((PEER_TOKEN_SPLIT, PEER_EXPERT_TILE, tm // PEER_TOKEN_SPLIT), BF16)],
        compiler_params=_cparams(("arbitrary", "arbitrary")),
        name="peer_ln2",
    )(x3, mod4, wq, kt, u_b, vt_b, ln_g, ln_b)


def _layer_weights(w_in, pool_w, w_branch_a, w_branch_b, w_out, peer_wq, peer_sub_keys, peer_u, peer_v,
                   *, pw, aw, iw, idim):
    d = w_in.shape[0]
    o = np.cumsum([0, pw, aw, aw, aw, iw, idim, IDX_HEADS, d, d])
    cut = lambda j: w_in[:, o[j]:o[j + 1]]
    u_w, q_w, k_w, v_w, qi_w, ki_w, wi_w, ga_w, gb_w = (cut(j) for j in range(9))
    wn = jnp.concatenate([u_w, k_w, v_w], axis=1).astype(BF16)
    wg = jnp.concatenate([ga_w, gb_w], axis=1).astype(BF16)
    wi_pad = jnp.pad(wi_w, ((0, 0), (0, 2 * SUBLANES - IDX_HEADS)))
    wt = jnp.concatenate([v_w, q_w, qi_w, wi_pad], axis=1).T.astype(BF16)
    n_keys, half = peer_sub_keys.shape[1], peer_sub_keys.shape[2]
    kt = jnp.einsum("hg,cb,ckd->hckgbd", jnp.eye(PEER_HEADS, dtype=F32), jnp.eye(2, dtype=F32), peer_sub_keys)
    kt = kt.reshape(PEER_HEADS * 2 * n_keys, PEER_HEADS * 2 * half).astype(BF16)
    return dict(wn=wn, wki=ki_w.astype(BF16), wg=wg, wt=wt, pool_w=pool_w.astype(BF16),
                wa=w_branch_a.astype(BF16), wb=w_branch_b.astype(BF16), wo=w_out.astype(BF16),
                wq=peer_wq.astype(BF16), kt=kt, u_b=peer_u.astype(BF16), vt_b=peer_v.T.astype(BF16))


def kernel(x_prompt, x_sample, cache_k, cache_v, cache_kidx, state_pool, page_table, c_prompt, c_sample,
           w_ada, b_ada, w_in, pool_w, pool_scale, w_branch_a, w_branch_b, w_out, ln1_g, ln1_b,
           peer_wq, peer_sub_keys, peer_u, peer_v, ln2_g, ln2_b):
    depth = w_ada.shape[0]
    b, s, d = x_prompt.shape
    db, t_new, _ = x_sample.shape
    page, n_heads, hd = cache_k.shape[2:]
    idim = cache_kidx.shape[3]
    pool_hist, pw = state_pool.shape[2:]
    n_pages = page_table.shape[1]
    past = n_pages * page
    aw, iw = n_heads * hd, IDX_HEADS * idim
    n_keys = peer_sub_keys.shape[2]
    alpha = (2 * depth) ** 0.25
    idx_scale = float(iw) ** -0.5
    q_scale = float(hd) ** -0.5
    assert t_new == SUBLANES and n_heads == SUBLANES and pool_hist < HIST_ROWS and s % TOKEN_TILE == 0
    assert q_scale == 2.0 ** round(math.log2(q_scale))

    n_c = b + db
    c_all = jnp.pad(jnp.concatenate([c_prompt, c_sample], axis=0), ((0, -n_c % SUBLANES), (0, 0)))
    y_p, y_s = x_prompt, x_sample
    k_t = jnp.transpose(cache_k, (0, 1, 3, 4, 2))
    v_t = jnp.transpose(cache_v, (0, 1, 3, 4, 2))
    ki_t = jnp.transpose(cache_kidx, (0, 1, 3, 2))
    outs = [[] for _ in range(8)]
    for l in range(depth):
        w = _layer_weights(w_in[l], pool_w[l], w_branch_a[l], w_branch_b[l], w_out[l], peer_wq[l], peer_sub_keys[l],
                           peer_u[l], peer_v[l], pw=pw, aw=aw, iw=iw, idim=idim)
        mod = _ada(c_all, w_ada[l], b_ada[l])
        mod_p, mod_s = mod[:, :b], mod[:, b:n_c]
        ps, g1, b1, g2, b2 = pool_scale[l][None], ln1_g[l][None], ln1_b[l][None], ln2_g[l][None], ln2_b[l][None]
        proj = functools.partial(_proj, wn=w["wn"], wki=w["wki"], wg=w["wg"], wt=w["wt"],
                                 pw=pw, aw=aw, iw=iw, idim=idim, q_scale=q_scale)

        x3, mod4, ks = _grouped(y_p, mod_p, TOKEN_TILE)
        u, k, v, ki, kb, kib, vt, qt, qit, wit, ga, gb = proj(x3, mod4, ks)
        nck = s // TOKEN_TILE
        a = _pool_prompt(u.reshape(b, s, pw), w["pool_w"], ps)
        o = _attn_prompt(qt.reshape(b, nck, aw, TOKEN_TILE), qit.reshape(b, nck, iw, TOKEN_TILE),
                         wit.reshape(b, nck, 2 * SUBLANES, TOKEN_TILE), kb.reshape(b, nck, TOKEN_TILE, aw),
                         vt.reshape(b, nck, aw, TOKEN_TILE), kib.reshape(b, nck, TOKEN_TILE, idim),
                         n_sel=min(TOPK_MAX, s // 4), idx_scale=idx_scale, n_heads=n_heads, hd=hd, idim=idim)
        x1 = _merge(x3, mod4, ks, a.reshape(b * s, pw), o.reshape(b * s, aw), ga, gb,
                    w["wa"], w["wb"], w["wo"], g1, b1, alpha)
        x3p, mod4p, ksp = _grouped(x1.reshape(b, s, d), mod_p, PEER_TOKEN_TILE)
        y_p = _peer(x3p, mod4p, ksp, w["wq"], w["kt"], w["u_b"], w["vt_b"], g2, b2, alpha, n_keys).reshape(b, s, d)
        outs[0].append(k.reshape(b, s, n_heads, hd))
        outs[1].append(v.reshape(b, s, n_heads, hd))
        outs[2].append(ki.reshape(b, s, idim))
        outs[3].append(u.reshape(b, s, pw)[:, s - pool_hist:])

        x3, mod4, ks = _grouped(y_s, mod_s, TOKEN_TILE)
        u, k, v, ki, kb, kib, vt, qt, qit, wit, ga, gb = proj(x3, mod4, ks)
        hist = state_pool[l]
        hist16 = jnp.pad(hist, ((0, 0), (HIST_ROWS - pool_hist, 0), (0, 0)))
        a = _pool_sample(u.reshape(db, t_new, pw), hist16, w["pool_w"], ps, past)
        untile = lambda z: jnp.moveaxis(z, 0, 1).reshape(z.shape[1], db, t_new)
        group = n_heads // 2
        q_s = jnp.transpose(untile(qt).reshape(2, group, hd, db, t_new), (3, 0, 1, 4, 2))
        qbd = jnp.einsum("bgltd,lm->bgltmd", q_s, jnp.eye(group, dtype=BF16)).reshape(db, 2, group * t_new, group * hd)
        qi2 = jnp.transpose(untile(qit).reshape(IDX_HEADS, idim, db, t_new), (2, 0, 3, 1)).reshape(db, IDX_HEADS * t_new, idim)
        wi3 = jnp.transpose(untile(wit)[:IDX_HEADS], (1, 0, 2))[..., None]
        lane_pad = lambda z: jnp.pad(z, ((0, 0),) * (z.ndim - 1) + ((0, page - t_new),))
        feat_major = lambda z: lane_pad(jnp.transpose(z.reshape(db, t_new, 2, group * hd), (0, 2, 3, 1)))
        knew = feat_major(kb)
        vnew = feat_major(v.astype(BF16))
        kinew = lane_pad(jnp.transpose(kib.reshape(db, t_new, idim), (0, 2, 1)))
        o = _attn_sample(page_table, qbd, qi2, wi3, knew, vnew, kinew, k_t, v_t, ki_t, l,
                         n_sel=min(TOPK_MAX, (past + t_new) // 4), idx_scale=idx_scale)
        o = jnp.transpose(o, (0, 2, 1, 3)).reshape(db * t_new, aw).astype(BF16)
        x1 = _merge(x3, mod4, ks, a, o, ga, gb, w["wa"], w["wb"], w["wo"], g1, b1, alpha)
        x3p, mod4p, ksp = _grouped(x1, mod_s, PEER_TOKEN_TILE)
        y_s = _peer(x3p, mod4p, ksp, w["wq"], w["kt"], w["u_b"], w["vt_b"], g2, b2, alpha, n_keys)
        outs[4].append(k.reshape(db, t_new, n_heads, hd))
        outs[5].append(v.reshape(db, t_new, n_heads, hd))
        outs[6].append(ki.reshape(db, t_new, idim))
        outs[7].append(jnp.concatenate([hist, u.reshape(db, t_new, pw)], axis=1)[:, t_new:])
    return (y_p, y_s) + tuple(jnp.stack(o_) for o_ in outs)
```

```python
import functools
import math

import jax
import jax.numpy as jnp
import numpy as np
from jax import lax
from jax.experimental import pallas as pl
from jax.experimental.pallas import tpu as pltpu

F32, BF16, I32 = jnp.float32, jnp.bfloat16, jnp.int32

POOL_WINDOWS = (2, 4, 8, 16)
IDX_HEADS = 8
TOPK_MAX = 256
PEER_HEADS = 8
PEER_TOPK = 16
LN_EPS = 1e-5
N_ADA = 6

VMEM_LIMIT_BYTES = 56 * 1024 * 1024
SUBLANES = 8
LANES = 128
BF16_ROWS = 16
TOKEN_TILE = 256
PEER_TOKEN_TILE = 512
PEER_EXPERT_TILE = 1024
PEER_TOKEN_SPLIT = 2
SAMPLE_SEARCH_BITS = 4
HIST_ROWS = 16

INT_MIN = -(2 ** 31)
KEY_NEG_INF = -2139095041
NEG_BIG = -1e30
LOG2E = 1.4426950408889634


def _cparams(sem):
    return pltpu.CompilerParams(dimension_semantics=sem, vmem_limit_bytes=VMEM_LIMIT_BYTES)


def _dot(a, b):
    return jnp.dot(a, b, preferred_element_type=F32)


def _dot_nt(a, b):
    return lax.dot_general(a, b, (((1,), (1,)), ((), ())), preferred_element_type=F32)


def _key_to_float(key):
    return lax.bitcast_convert_type(key ^ ((key >> 31) & 0x7FFFFFFF), F32)


def _threshold_step(i, key, count_ge, n_sel, bits=1):
    unit = lax.shift_left(jnp.int32(1), 32 - bits * (i + 1))
    digit = jnp.zeros_like(key)
    for j in range(1, 2 ** bits):
        trial = key + j * unit
        ok = jnp.logical_or(count_ge(_key_to_float(trial)) >= n_sel, trial < KEY_NEG_INF)
        digit = digit + jnp.where(ok, 1, 0)
    return key + digit * unit


def _ada_kernel(c_ref, w_ref, b_ref, o_ref):
    o_ref[...] = _dot(c_ref[...].astype(BF16), w_ref[...].astype(BF16)) + b_ref[...]


def _ada(c_all, w_ada, b_ada):
    n, d = c_all.shape
    return pl.pallas_call(
        _ada_kernel,
        grid=(N_ADA,),
        in_specs=[pl.BlockSpec((n, d), lambda j: (0, 0)),
                  pl.BlockSpec((d, d), lambda j: (0, j)),
                  pl.BlockSpec((1, d), lambda j: (0, j))],
        out_specs=pl.BlockSpec((None, n, d), lambda j: (j, 0, 0)),
        out_shape=jax.ShapeDtypeStruct((N_ADA, n, d), F32),
        compiler_params=_cparams(("arbitrary",)),
        name="ada",
    )(c_all, w_ada, b_ada.reshape(1, -1))


def _grouped(x, mod, rows):
    nseq, l, d = x.shape
    if l >= rows:
        assert l % rows == 0
        per = l // rows
        x3 = x.reshape(nseq * per, rows, d)
        mod4 = jnp.broadcast_to(mod[:, :, None, None, :], (N_ADA, nseq, per, 1, d)).reshape(N_ADA, nseq * per, 1, d)
        return x3, mod4, 1
    assert rows % l == 0 and nseq % (rows // l) == 0
    return x, mod[:, :, None, :], rows // l


def _proj_kernel(x_ref, mod_ref, wn_ref, wki_ref, wg_ref, wt_ref,
                 u_ref, k_ref, v_ref, ki_ref, kb_ref, kib_ref, vt_ref, qt_ref, qit_ref, wit_ref, ga_ref, gb_ref,
                 *, pw, aw, iw, q_scale):
    ks, r, d = x_ref.shape
    h = (x_ref[...] * (1.0 + mod_ref[1]) + mod_ref[0]).reshape(ks * r, d).astype(BF16)
    u_ref[...] = _dot(h, wn_ref[:, 0:pw])
    k = _dot(h, wn_ref[:, pw:pw + aw])
    k_ref[...] = k
    kb_ref[...] = k.astype(BF16)
    v_ref[...] = _dot(h, wn_ref[:, pw + aw:pw + 2 * aw])
    ki = _dot(h, wki_ref[...])
    ki_ref[...] = ki
    kib_ref[...] = ki.astype(BF16)
    ga_ref[...] = _dot(h, wg_ref[:, 0:d])
    gb_ref[...] = _dot(h, wg_ref[:, d:2 * d])
    vt_ref[...] = _dot_nt(wt_ref[0:aw, :], h).astype(BF16)
    qt_ref[...] = (_dot_nt(wt_ref[aw:2 * aw, :], h) * q_scale).astype(BF16)
    qit_ref[...] = _dot_nt(wt_ref[2 * aw:2 * aw + iw, :], h).astype(BF16)
    wit_ref[...] = _dot_nt(wt_ref[2 * aw + iw:2 * aw + iw + 2 * SUBLANES, :], h)


def _proj(x3, mod4, ks, wn, wki, wg, wt, *, pw, aw, iw, idim, q_scale):
    g, r, d = x3.shape
    tm = ks * r
    nb = g // ks
    n = g * r
    row = lambda cols: pl.BlockSpec((tm, cols), lambda i: (i, 0))
    col = lambda rows: pl.BlockSpec((None, rows, tm), lambda i: (i, 0, 0))
    full = lambda a: pl.BlockSpec(a.shape, lambda i: (0,) * a.ndim)
    outs = [((n, pw), F32, row(pw)), ((n, aw), F32, row(aw)), ((n, aw), F32, row(aw)), ((n, idim), F32, row(idim)),
            ((n, aw), BF16, row(aw)), ((n, idim), BF16, row(idim)),
            ((nb, aw, tm), BF16, col(aw)), ((nb, aw, tm), BF16, col(aw)), ((nb, iw, tm), BF16, col(iw)),
            ((nb, 2 * SUBLANES, tm), F32, col(2 * SUBLANES)),
            ((n, d), F32, row(d)), ((n, d), F32, row(d))]
    return pl.pallas_call(
        functools.partial(_proj_kernel, pw=pw, aw=aw, iw=iw, q_scale=q_scale),
        grid=(nb,),
        in_specs=[pl.BlockSpec((ks, r, d), lambda i: (i, 0, 0)),
                  pl.BlockSpec((N_ADA, ks, 1, d), lambda i: (0, i, 0, 0)),
                  full(wn), full(wki), full(wg), full(wt)],
        out_specs=[o[2] for o in outs],
        out_shape=[jax.ShapeDtypeStruct(o[0], o[1]) for o in outs],
        compiler_params=_cparams(("arbitrary",)),
        name="in_proj",
    )(x3, mod4, wn, wki, wg, wt)


def _pool_windows(ext_ref, pos, pw_ref, ps_ref, a_ref):
    ks, rows, width = ext_ref.shape
    r = rows - HIST_ROWS
    gw = width // len(POOL_WINDOWS)
    for g, w in enumerate(POOL_WINDOWS):
        lo = g * gw
        cur = ext_ref[:, HIST_ROWS:HIST_ROWS + r, lo:lo + gw]
        acc = cur
        for j in range(1, w):
            acc = acc + ext_ref[:, HIST_ROWS - j:HIST_ROWS - j + r, lo:lo + gw]
        cnt = jnp.minimum(pos + 1, w).astype(F32)
        pooled = (acc / cnt - cur).reshape(ks * r, gw).astype(BF16)
        mixed = _dot(pooled, pw_ref[g])
        a_ref[:, lo:lo + gw] = (mixed * ps_ref[:, lo:lo + gw]).astype(BF16)


def _pool_prompt_kernel(u_ref, pw_ref, ps_ref, a_ref, ext_ref):
    i = pl.program_id(1)
    tp, width = u_ref.shape
    gw = width // len(POOL_WINDOWS)

    @pl.when(i == 0)
    def _():
        ext_ref[:, 0:HIST_ROWS, :] = jnp.zeros((1, HIST_ROWS, width), F32)

    @pl.when(i > 0)
    def _():
        ext_ref[:, 0:HIST_ROWS, :] = ext_ref[:, tp:tp + HIST_ROWS, :]

    ext_ref[:, HIST_ROWS:, :] = u_ref[...][None]
    pos = i * tp + lax.broadcasted_iota(I32, (1, tp, gw), 1)
    _pool_windows(ext_ref, pos, pw_ref, ps_ref, a_ref)


def _pool_prompt(u, pool_w, pool_scale):
    b, s, width = u.shape
    tp = TOKEN_TILE
    return pl.pallas_call(
        _pool_prompt_kernel,
        grid=(b, s // tp),
        in_specs=[pl.BlockSpec((None, tp, width), lambda bi, i: (bi, i, 0)),
                  pl.BlockSpec(pool_w.shape, lambda bi, i: (0, 0, 0)),
                  pl.BlockSpec(pool_scale.shape, lambda bi, i: (0, 0))],
        out_specs=pl.BlockSpec((None, tp, width), lambda bi, i: (bi, i, 0)),
        out_shape=jax.ShapeDtypeStruct((b, s, width), BF16),
        scratch_shapes=[pltpu.VMEM((1, HIST_ROWS + tp, width), F32)],
        compiler_params=_cparams(("arbitrary", "arbitrary")),
        name="pool_prompt",
    )(u, pool_w, pool_scale)


def _pool_sample_kernel(u_ref, hist_ref, pw_ref, ps_ref, a_ref, ext_ref, *, pos0):
    ks, t, width = u_ref.shape
    gw = width // len(POOL_WINDOWS)
    ext_ref[:, 0:HIST_ROWS, :] = hist_ref[...]
    ext_ref[:, HIST_ROWS:, :] = u_ref[...]
    pos = pos0 + lax.broadcasted_iota(I32, (1, t, gw), 1)
    _pool_windows(ext_ref, pos, pw_ref, ps_ref, a_ref)


def _pool_sample(u, hist16, pool_w, pool_scale, pos0):
    db, t, width = u.shape
    ks = TOKEN_TILE // t
    return pl.pallas_call(
        functools.partial(_pool_sample_kernel, pos0=pos0),
        grid=(db // ks,),
        in_specs=[pl.BlockSpec((ks, t, width), lambda i: (i, 0, 0)),
                  pl.BlockSpec((ks, HIST_ROWS, width), lambda i: (i, 0, 0)),
                  pl.BlockSpec(pool_w.shape, lambda i: (0, 0, 0)),
                  pl.BlockSpec(pool_scale.shape, lambda i: (0, 0))],
        out_specs=pl.BlockSpec((ks * t, width), lambda i: (i, 0)),
        out_shape=jax.ShapeDtypeStruct((db * t, width), BF16),
        scratch_shapes=[pltpu.VMEM((ks, HIST_ROWS + t, width), F32)],
        compiler_params=_cparams(("arbitrary",)),
        name="pool_sample",
    )(u, hist16, pool_w, pool_scale)


def _alibi_slopes(n_heads):
    return [2.0 ** (-8.0 * (h + 1) / n_heads) for h in range(n_heads)]


def _attn_prompt_kernel(qt_ref, qit_ref, wit_ref, kb_ref, vt_ref, kib_ref, o_ref,
                        sc_sc, qz_sc, acc_sc, lg_sc, al_sc, *, n_sel, idx_scale, n_heads, hd, idim, seq_len):
    qb = pl.program_id(1)
    tq = qt_ref.shape[-1]
    sk = tq
    nchunk = qb + 1
    t_idx = qb * tq + lax.broadcasted_iota(I32, (sk, tq), 1)
    s_loc = lax.broadcasted_iota(I32, (sk, tq), 0)

    def score_body(c, carry):
        kic = kib_ref[c]
        acc = jnp.zeros((sk, tq), F32)
        for h in range(IDX_HEADS):
            rel = _dot(kic, qit_ref[h * idim:(h + 1) * idim, :])
            acc = acc + wit_ref[h:h + 1, :] * jnp.maximum(rel, 0.0)
        sc_sc[c] = jnp.where(c * sk + s_loc <= t_idx, acc * idx_scale, -jnp.inf)
        return carry

    lax.fori_loop(0, nchunk, score_body, 0)

    def colsum(mask):
        return jnp.where(mask, 1.0, 0.0).reshape(sk // SUBLANES, SUBLANES, tq).sum(axis=0)

    @pl.when(nchunk % 2 == 1)
    def _():
        sc_sc[nchunk] = jnp.full((sk, tq), -jnp.inf, F32)

    def count(pred):
        def pair(c2, a):
            c = 2 * c2
            return a + colsum(pred(c, sc_sc[c])) + colsum(pred(c + 1, sc_sc[c + 1]))
        acc = lax.fori_loop(0, (nchunk + 1) // 2, pair, jnp.zeros((SUBLANES, tq), F32))
        return acc.sum(axis=0, keepdims=True)

    count_ge = lambda t: count(lambda c, s: s >= t)
    tau_key = lax.fori_loop(0, 32, lambda i, k: _threshold_step(i, k, count_ge, n_sel),
                            jnp.full((1, tq), INT_MIN, I32))
    tau = _key_to_float(tau_key)

    tie_rows = jnp.logical_and(count_ge(tau) > n_sel, tau_key > KEY_NEG_INF)
    has_tie = jnp.max(jnp.where(tie_rows, 1.0, 0.0)) > 0.5

    @pl.when(has_tie)
    def _():
        need = n_sel - count(lambda c, s: s > tau)
        nbits = max(1, int(math.ceil(math.log2(seq_len))))

        def first_enough(i, j):
            step = lax.shift_left(jnp.int32(1), nbits - 1 - i)
            upto = count(lambda c, s: jnp.logical_and(s == tau, c * sk + s_loc <= j + step - 1))
            return jnp.where(upto < need, j + step, j)

        last = lax.fori_loop(0, nbits, first_enough, jnp.zeros((1, tq), I32))

        def demote(c, carry):
            s = sc_sc[c]
            drop = jnp.logical_and(jnp.logical_and(tie_rows, s == tau), c * sk + s_loc > last)
            sc_sc[c] = jnp.where(drop, -jnp.inf, s)
            return carry

        lax.fori_loop(0, nchunk, demote, 0)

    pair_rows = lax.broadcasted_iota(I32, (2 * hd, tq), 0)
    for h in range(n_heads):
        blk = qt_ref[(h // 2) * 2 * hd:(h // 2 + 1) * 2 * hd, :]
        keep = pair_rows < hd if h % 2 == 0 else pair_rows >= hd
        qz_sc[h] = jnp.where(keep, blk, jnp.zeros_like(blk))
    acc_sc[...] = jnp.zeros(acc_sc.shape, F32)
    slopes = [s * LOG2E for s in _alibi_slopes(n_heads)]
    for h in range(n_heads):
        al_sc[h] = slopes[h] * s_loc.astype(F32)

    def att_chunk(c, j, m_all, l_all):
        sel = jnp.logical_and(sc_sc[c] >= tau, c * sk + s_loc <= t_idx)
        off = ((c - qb) * sk).astype(F32)
        m_out, l_out = [], []
        for h in range(n_heads):
            lg = jnp.where(sel, lg_sc[j, h] + al_sc[h], -jnp.inf)
            m_new = jnp.maximum(m_all[h], jnp.max(lg, axis=0, keepdims=True) + slopes[h] * off)
            alpha = jnp.exp2(m_all[h] - m_new)
            p = jnp.exp2(lg - (m_new - slopes[h] * off))
            l_out.append(alpha * l_all[h] + jnp.sum(p, axis=0, keepdims=True))
            m_out.append(m_new)
            pv = _dot(vt_ref[c, h * hd:(h + 1) * hd, :], p.astype(BF16))
            acc_sc[h * hd:(h + 1) * hd, :] = alpha * acc_sc[h * hd:(h + 1) * hd, :] + pv
        return tuple(m_out), tuple(l_out)

    def att_pair(c2, carry):
        for j in range(2):
            for h in range(n_heads):
                lg_sc[j, h] = _dot(kb_ref[2 * c2 + j, :, (h // 2) * 2 * hd:(h // 2 + 1) * 2 * hd], qz_sc[h])
        for j in range(2):
            carry = att_chunk(2 * c2 + j, j, *carry)
        return carry

    init = (tuple(jnp.full((1, tq), NEG_BIG, F32) for _ in range(n_heads)),
            tuple(jnp.zeros((1, tq), F32) for _ in range(n_heads)))
    _, l_all = lax.fori_loop(0, (nchunk + 1) // 2, att_pair, init)
    out_t = jnp.concatenate([acc_sc[h * hd:(h + 1) * hd, :] / l_all[h] for h in range(n_heads)], axis=0)
    o_ref[...] = out_t.T.astype(BF16)


def _attn_prompt(qt, qit, wit, kb, vt, kib, *, n_sel, idx_scale, n_heads, hd, idim):
    b, nck, aw, tq = qt.shape
    assert nck % 2 == 0
    iw = qit.shape[2]
    resident = lambda shape: pl.BlockSpec((None,) + shape, lambda bi, i: (bi, 0, 0, 0), pipeline_mode=pl.Buffered(1))
    tile = lambda rows: pl.BlockSpec((None, None, rows, tq), lambda bi, i: (bi, i, 0, 0))
    return pl.pallas_call(
        functools.partial(_attn_prompt_kernel, n_sel=n_sel, idx_scale=idx_scale, n_heads=n_heads, hd=hd, idim=idim,
                          seq_len=nck * tq),
        grid=(b, nck),
        in_specs=[tile(aw), tile(iw), tile(2 * SUBLANES),
                  resident((nck, tq, aw)), resident((nck, aw, tq)), resident((nck, tq, idim))],
        out_specs=pl.BlockSpec((None, tq, aw), lambda bi, i: (bi, i, 0)),
        out_shape=jax.ShapeDtypeStruct((b, nck * tq, aw), BF16),
        scratch_shapes=[pltpu.VMEM((nck, tq, tq), F32),
                        pltpu.VMEM((n_heads, 2 * hd, tq), BF16),
                        pltpu.VMEM((aw, tq), F32),
                        pltpu.VMEM((2, n_heads, tq, tq), F32),
                        pltpu.VMEM((n_heads, tq, tq), F32)],
        compiler_params=_cparams(("arbitrary", "arbitrary")),
        name="attn_prompt",
    )(qt, qit, wit, kb, vt, kib)


def _attn_sample_kernel(pt_ref, qbd_ref, qi_ref, wi_ref, knew_ref, vnew_ref, kinew_ref, *rest,
                        n_pages, page, n_sel, idx_scale, n_heads, hd, t_new):
    del pt_ref
    ki_refs, k_refs, v_refs = rest[0:n_pages], rest[n_pages:2 * n_pages], rest[2 * n_pages:3 * n_pages]
    o_ref, sc_sc, lg_sc = rest[3 * n_pages:]
    past = n_pages * page
    width = past + page
    group = n_heads // 2
    row = lax.broadcasted_iota(I32, (t_new, page), 0)
    lane = lax.broadcasted_iota(I32, (t_new, page), 1)
    wi = wi_ref[...]

    def scores(kit):
        rel = jnp.maximum(_dot(qi_ref[...], kit), 0.0)
        return jnp.sum(rel.reshape(IDX_HEADS, t_new, rel.shape[1]) * wi, axis=0) * idx_scale

    def page_pair(refs, pp, head_lo=None):
        if head_lo is None:
            parts = [refs[2 * pp + j][...] for j in range(2)]
        else:
            parts = [refs[2 * pp + j][head_lo:head_lo + group].reshape(group * hd, page) for j in range(2)]
        return jnp.concatenate(parts, axis=1).astype(BF16)

    for pp in range(n_pages // 2):
        sc_sc[:, 2 * pp * page:(2 * pp + 2) * page] = scores(page_pair(ki_refs, pp))
    sc_sc[:, past:width] = jnp.where(lane <= row, scores(kinew_ref[...]), -jnp.inf)

    def count(pred):
        return jnp.sum(jnp.where(pred(sc_sc[...]), 1.0, 0.0), axis=1, keepdims=True)

    count_ge = lambda t: count(lambda s: s >= t)
    tau_key = lax.fori_loop(0, 32 // SAMPLE_SEARCH_BITS,
                            lambda i, k: _threshold_step(i, k, count_ge, n_sel, SAMPLE_SEARCH_BITS),
                            jnp.full((t_new, 1), INT_MIN, I32))
    tau = _key_to_float(tau_key)

    tie_rows = jnp.logical_and(count_ge(tau) > n_sel, tau_key > KEY_NEG_INF)
    has_tie = jnp.max(jnp.where(tie_rows, 1.0, 0.0)) > 0.5
    pos = lax.broadcasted_iota(I32, (t_new, width), 1)

    @pl.when(has_tie)
    def _():
        need = n_sel - count(lambda s: s > tau)
        nbits = max(1, int(math.ceil(math.log2(width))))

        def first_enough(i, j):
            step = lax.shift_left(jnp.int32(1), nbits - 1 - i)
            upto = count(lambda s: jnp.logical_and(s == tau, pos <= j + step - 1))
            return jnp.where(upto < need, j + step, j)

        last = lax.fori_loop(0, nbits, first_enough, jnp.zeros((t_new, 1), I32))
        s = sc_sc[...]
        drop = jnp.logical_and(jnp.logical_and(tie_rows, s == tau), pos > last)
        sc_sc[...] = jnp.where(drop, -jnp.inf, s)

    slopes = [s * LOG2E for s in _alibi_slopes(n_heads)]
    t_pos = past + lax.broadcasted_iota(I32, (t_new, width), 0)
    sel = jnp.logical_and(sc_sc[...] >= tau, pos <= t_pos)
    bias = jnp.where(sel, 0.0, -jnp.inf)
    dist = (t_pos - pos).astype(F32)
    spans = [(2 * pp * page, 2 * page) for pp in range(n_pages // 2)] + [(past, page)]
    for g in range(2):
        for pp, (lo, n) in enumerate(spans):
            kt = page_pair(k_refs, pp, g * group) if pp < n_pages // 2 else knew_ref[g]
            lg = _dot(qbd_ref[g], kt).reshape(group, t_new, n)
            for hl in range(group):
                h = g * group + hl
                lg_sc[h, :, lo:lo + n] = lg[hl] - slopes[h] * dist[:, lo:lo + n] + bias[:, lo:lo + n]
    for g in range(2):
        probs, denoms = [], []
        for hl in range(group):
            lg = lg_sc[g * group + hl]
            pr = jnp.exp2(lg - jnp.max(lg, axis=1, keepdims=True))
            denoms.append(jnp.sum(pr, axis=1, keepdims=True))
            probs.append(pr)
        pg = jnp.concatenate(probs, axis=0).astype(BF16)
        acc = jnp.zeros((group * t_new, group * hd), F32)
        for pp, (lo, n) in enumerate(spans):
            vt = page_pair(v_refs, pp, g * group) if pp < n_pages // 2 else vnew_ref[g]
            acc = acc + _dot_nt(pg[:, lo:lo + n], vt)
        for hl in range(group):
            o_ref[g * group + hl] = acc[hl * t_new:(hl + 1) * t_new, hl * hd:(hl + 1) * hd] / denoms[hl]


def _attn_sample(page_table, qbd, qi2, wi3, knew, vnew, kinew, k_t, v_t, ki_t, layer, *, n_sel, idx_scale):
    db = qbd.shape[0]
    n_pages = page_table.shape[1]
    n_heads, hd, page = k_t.shape[2:]
    idim = ki_t.shape[2]
    t_new = wi3.shape[2]
    assert n_pages % 2 == 0 and n_heads % 2 == 0
    per_seq = lambda a: pl.BlockSpec((None,) + a.shape[1:], lambda b, pt: (b,) + (0,) * (a.ndim - 1))
    kv_page = lambda j: pl.BlockSpec((None, None, n_heads, hd, page), lambda b, pt: (layer, pt[b, j], 0, 0, 0))
    ki_page = lambda j: pl.BlockSpec((None, None, idim, page), lambda b, pt: (layer, pt[b, j], 0, 0))
    in_specs = ([per_seq(qbd), per_seq(qi2), per_seq(wi3), per_seq(knew), per_seq(vnew), per_seq(kinew)]
                + [ki_page(j) for j in range(n_pages)]
                + [kv_page(j) for j in range(n_pages)]
                + [kv_page(j) for j in range(n_pages)])
    grid_spec = pltpu.PrefetchScalarGridSpec(
        num_scalar_prefetch=1,
        grid=(db,),
        in_specs=in_specs,
        out_specs=pl.BlockSpec((None, n_heads, t_new, hd), lambda b, pt: (b, 0, 0, 0)),
        scratch_shapes=[pltpu.VMEM((t_new, (n_pages + 1) * page), F32),
                        pltpu.VMEM((n_heads, t_new, (n_pages + 1) * page), F32)],
    )
    return pl.pallas_call(
        functools.partial(_attn_sample_kernel, n_pages=n_pages, page=page, n_sel=n_sel, idx_scale=idx_scale,
                          n_heads=n_heads, hd=hd, t_new=t_new),
        grid_spec=grid_spec,
        out_shape=jax.ShapeDtypeStruct((db, n_heads, t_new, hd), F32),
        compiler_params=_cparams(("arbitrary",)),
        name="attn_sample",
    )(page_table, qbd, qi2, wi3, knew, vnew, kinew, *([ki_t] * n_pages), *([k_t] * n_pages), *([v_t] * n_pages))


def _layer_norm(y, g, b):
    mu = jnp.mean(y, axis=-1, keepdims=True)
    var = jnp.mean(jnp.square(y - mu), axis=-1, keepdims=True)
    return (y - mu) * lax.rsqrt(var + LN_EPS) * g + b


def _merge_kernel(x_ref, mod_ref, a_ref, o_ref, ga_ref, gb_ref, wa_ref, wb_ref, wo_ref, g_ref, b_ref, x1_ref, *, alpha):
    ks, r, d = x_ref.shape
    merged = (jax.nn.sigmoid(ga_ref[...]) * _dot(a_ref[...], wa_ref[...])
              + jax.nn.sigmoid(gb_ref[...]) * _dot(o_ref[...], wb_ref[...]))
    upd = _dot(merged.astype(BF16), wo_ref[...]).reshape(ks, r, d)
    y = alpha * x_ref[...] + mod_ref[2] * upd
    x1_ref[...] = _layer_norm(y, g_ref[...], b_ref[...])


def _merge(x3, mod4, ks, a, o, ga, gb, wa, wb, wo, ln_g, ln_b, alpha):
    g, r, d = x3.shape
    tm = ks * r
    row = lambda arr: pl.BlockSpec((tm, arr.shape[1]), lambda i: (i, 0))
    full = lambda arr: pl.BlockSpec(arr.shape, lambda i: (0,) * arr.ndim)
    return pl.pallas_call(
        functools.partial(_merge_kernel, alpha=alpha),
        grid=(g // ks,),
        in_specs=[pl.BlockSpec((ks, r, d), lambda i: (i, 0, 0)),
                  pl.BlockSpec((N_ADA, ks, 1, d), lambda i: (0, i, 0, 0)),
                  row(a), row(o), row(ga), row(gb), full(wa), full(wb), full(wo), full(ln_g), full(ln_b)],
        out_specs=pl.BlockSpec((ks, r, d), lambda i: (i, 0, 0)),
        out_shape=jax.ShapeDtypeStruct((g, r, d), F32),
        compiler_params=_cparams(("arbitrary",)),
        name="merge_ln1",
    )(x3, mod4, a, o, ga, gb, wa, wb, wo, ln_g, ln_b)


def _top_rows_exact(s, n_top):
    rows = s.shape[0]
    row_id = lax.broadcasted_iota(I32, s.shape, 0)
    rank = jnp.full(s.shape, n_top, I32)
    vals = []
    for r in range(n_top):
        m = jnp.max(s, axis=0, keepdims=True)
        idx = jnp.min(jnp.where(s == m, row_id, rows), axis=0, keepdims=True)
        hit = row_id == idx
        vals.append(m)
        rank = jnp.where(hit, r, rank)
        s = jnp.where(hit, -jnp.inf, s)
    return jnp.concatenate(vals, axis=0), rank


def _top_rows_distinct(s, n_top):
    rank = jnp.full(s.shape, n_top, I32)
    vals = []
    for r in range(n_top):
        m = jnp.max(s, axis=0, keepdims=True)
        hit = s == m
        vals.append(m)
        rank = jnp.where(hit, r, rank)
        s = jnp.where(hit, -jnp.inf, s)
    return jnp.concatenate(vals, axis=0), rank


def _rank_ties(rank, n_top):
    taken = jnp.sum(jnp.where(rank < n_top, 1.0, 0.0), axis=0, keepdims=True)
    return jnp.max(jnp.where(taken != n_top, 1.0, 0.0)) > 0.5


PEER_PAIRS = [(a, b) for a in range(PEER_TOPK) for b in range(PEER_TOPK) if (a + 1) * (b + 1) <= PEER_TOPK]
PEER_CAND_ROWS = -(-len(PEER_PAIRS) // SUBLANES) * SUBLANES


def _peer_kernel(x_ref, mod_ref, wq_ref, kt_ref, u_ref, vt_ref, g_ref, b_ref, y_ref,
                 h2_sc, st_sc, e0_sc, nb_sc, e1_sc, r1_sc, acc_sc, sv_sc, rk_sc, ts_sc, r2_sc, act_sc, w_sc,
                 *, alpha, n_keys):
    ks, r, d = x_ref.shape
    tm = ks * r
    step = pl.program_id(1)
    sub = PEER_EXPERT_TILE // n_keys

    @pl.when(step == 0)
    def _():
        h2 = (x_ref[...] * (1.0 + mod_ref[4]) + mod_ref[3]).reshape(tm, d).astype(BF16)
        h2_sc[...] = h2
        qp = _dot(h2, wq_ref[...]).astype(BF16)
        st_sc[...] = _dot_nt(kt_ref[...], qp)
        acc_sc[...] = jnp.zeros(acc_sc.shape, F32)

        def route(h, carry):
            s0 = st_sc[pl.ds(pl.multiple_of(h * 2 * n_keys, 2 * n_keys), n_keys), :]
            s1 = st_sc[pl.ds(pl.multiple_of(h * 2 * n_keys + n_keys, n_keys), n_keys), :]

            def halves(top):
                sv_sc[0], rk_sc[0] = top(s0, PEER_TOPK)
                sv_sc[1], rk_sc[1] = top(s1, PEER_TOPK)

            halves(_top_rows_distinct)

            @pl.when(jnp.logical_or(_rank_ties(rk_sc[0], PEER_TOPK), _rank_ties(rk_sc[1], PEER_TOPK)))
            def _():
                halves(_top_rows_exact)

            v0, v1, rank0 = sv_sc[0], sv_sc[1], rk_sc[0]
            pad = [jnp.full((PEER_CAND_ROWS - len(PEER_PAIRS), tm), -jnp.inf, F32)]
            cand = jnp.concatenate([v0[a:a + 1] + v1[b:b + 1] for a, b in PEER_PAIRS] + pad, axis=0)

            def pairs(top):
                ts_sc[...], r2_sc[...] = top(cand, PEER_TOPK)

            pairs(_top_rows_distinct)

            @pl.when(_rank_ties(r2_sc[...], PEER_TOPK))
            def _():
                pairs(_top_rows_exact)

            top_s, rank2 = ts_sc[...], r2_sc[...]
            z = jnp.sum(jnp.exp(top_s - top_s[0:1]), axis=0, keepdims=True)
            took = jnp.where(rank2 < PEER_TOPK, 1, 0)
            take = jnp.zeros(rank0.shape, I32)
            for a in range(PEER_TOPK):
                n_a = sum(took[i:i + 1] for i, (pa, _) in enumerate(PEER_PAIRS) if pa == a)
                take = jnp.where(rank0 == a, n_a, take)
            e0_sc[h] = (jnp.exp(s0 - v0[0:1]) / z).reshape(n_keys // SUBLANES, SUBLANES, tm)
            nb_sc[h] = take.astype(F32).reshape(n_keys // SUBLANES, SUBLANES, tm)
            e1_sc[h] = jnp.exp(s1 - v1[0:1]).astype(BF16).reshape(n_keys // BF16_ROWS, BF16_ROWS, tm)
            r1_sc[h] = rk_sc[1].astype(F32).astype(BF16).reshape(n_keys // BF16_ROWS, BF16_ROWS, tm)
            return carry

        lax.fori_loop(0, PEER_HEADS, route, 0)

    n_tok = tm // PEER_TOKEN_SPLIT
    for t in range(PEER_TOKEN_SPLIT):
        act_sc[t] = _dot_nt(u_ref[...], h2_sc[t * n_tok:(t + 1) * n_tok, :])
    for t in range(PEER_TOKEN_SPLIT):
        lanes = slice(t * n_tok, (t + 1) * n_tok)
        for ii in range(sub):
            act = act_sc[t, ii * n_keys:(ii + 1) * n_keys, :]
            gate = jnp.zeros((n_keys // BF16_ROWS, BF16_ROWS, n_tok), BF16)
            for h in range(PEER_HEADS):
                take_i = jnp.broadcast_to(nb_sc[h, step, ii:ii + 1, lanes], (BF16_ROWS, n_tok)).astype(BF16)
                e0_i = jnp.broadcast_to(e0_sc[h, step, ii:ii + 1, lanes], (BF16_ROWS, n_tok)).astype(BF16)
                e1 = e1_sc[h, :, :, lanes]
                gate = gate + jnp.where(r1_sc[h, :, :, lanes] < take_i, e1, jnp.zeros_like(e1)) * e0_i
            gelu = 0.5 * act * (1.0 + lax.erf(act * (2.0 ** -0.5)))
            w = gelu.astype(BF16).reshape(n_keys // BF16_ROWS, BF16_ROWS, n_tok) * gate
            w_sc[t, ii * n_keys:(ii + 1) * n_keys, :] = w.reshape(n_keys, n_tok)
    for t in range(PEER_TOKEN_SPLIT):
        acc_sc[:, t * n_tok:(t + 1) * n_tok] += _dot(vt_ref[...], w_sc[t])

    @pl.when(step == pl.num_programs(1) - 1)
    def _():
        f = acc_sc[...].T.reshape(ks, r, d)
        y = alpha * x_ref[...] + mod_ref[5] * f
        y_ref[...] = _layer_norm(y, g_ref[...], b_ref[...])


def _peer(x3, mod4, ks, wq, kt, u_b, vt_b, ln_g, ln_b, alpha, n_keys):
    g, r, d = x3.shape
    tm = ks * r
    n_exp = u_b.shape[0]
    assert PEER_EXPERT_TILE == SUBLANES * n_keys and n_exp % PEER_EXPERT_TILE == 0
    full = lambda arr: pl.BlockSpec(arr.shape, lambda i, e: (0,) * arr.ndim)
    return pl.pallas_call(
        functools.partial(_peer_kernel, alpha=alpha, n_keys=n_keys),
        grid=(g // ks, n_exp // PEER_EXPERT_TILE),
        in_specs=[pl.BlockSpec((ks, r, d), lambda i, e: (i, 0, 0)),
                  pl.BlockSpec((N_ADA, ks, 1, d), lambda i, e: (0, i, 0, 0)),
                  full(wq), full(kt),
                  pl.BlockSpec((PEER_EXPERT_TILE, d), lambda i, e: (e, 0)),
                  pl.BlockSpec((d, PEER_EXPERT_TILE), lambda i, e: (0, e)),
                  full(ln_g), full(ln_b)],
        out_specs=pl.BlockSpec((ks, r, d), lambda i, e: (i, 0, 0)),
        out_shape=jax.ShapeDtypeStruct((g, r, d), F32),
        scratch_shapes=[pltpu.VMEM((tm, d), BF16),
                        pltpu.VMEM((kt.shape[0], tm), F32),
                        pltpu.VMEM((PEER_HEADS, n_keys // SUBLANES, SUBLANES, tm), F32),
                        pltpu.VMEM((PEER_HEADS, n_keys // SUBLANES, SUBLANES, tm), F32),
                        pltpu.VMEM((PEER_HEADS, n_keys // BF16_ROWS, BF16_ROWS, tm), BF16),
                        pltpu.VMEM((PEER_HEADS, n_keys // BF16_ROWS, BF16_ROWS, tm), BF16),
                        pltpu.VMEM((d, tm), F32),
                        pltpu.VMEM((2, PEER_TOPK, tm), F32),
                        pltpu.VMEM((2, n_keys, tm), I32),
                        pltpu.VMEM((PEER_TOPK, tm), F32),
                        pltpu.VMEM((PEER_CAND_ROWS, tm), I32),
                        pltpu.VMEM((PEER_TOKEN_SPLIT, PEER_EXPERT_TILE, tm // PEER_TOKEN_SPLIT), F32),
                        pltpu.VMEM((PEER_TOKEN_SPLIT, PEER_EXPERT_TILE, tm // PEER_TOKEN_SPLIT), BF16)],
        compiler_params=_cparams(("arbitrary", "arbitrary")),
        name="peer_ln2",
    )(x3, mod4, wq, kt, u_b, vt_b, ln_g, ln_b)


def _layer_weights(w_in, pool_w, w_branch_a, w_branch_b, w_out, peer_wq, peer_sub_keys, peer_u, peer_v,
                   *, pw, aw, iw, idim):
    d = w_in.shape[0]
    o = np.cumsum([0, pw, aw, aw, aw, iw, idim, IDX_HEADS, d, d])
    cut = lambda j: w_in[:, o[j]:o[j + 1]]
    u_w, q_w, k_w, v_w, qi_w, ki_w, wi_w, ga_w, gb_w = (cut(j) for j in range(9))
    wn = jnp.concatenate([u_w, k_w, v_w], axis=1).astype(BF16)
    wg = jnp.concatenate([ga_w, gb_w], axis=1).astype(BF16)
    wi_pad = jnp.pad(wi_w, ((0, 0), (0, 2 * SUBLANES - IDX_HEADS)))
    wt = jnp.concatenate([v_w, q_w, qi_w, wi_pad], axis=1).T.astype(BF16)
    n_keys, half = peer_sub_keys.shape[1], peer_sub_keys.shape[2]
    kt = jnp.einsum("hg,cb,ckd->hckgbd", jnp.eye(PEER_HEADS, dtype=F32), jnp.eye(2, dtype=F32), peer_sub_keys)
    kt = kt.reshape(PEER_HEADS * 2 * n_keys, PEER_HEADS * 2 * half).astype(BF16)
    return dict(wn=wn, wki=ki_w.astype(BF16), wg=wg, wt=wt, pool_w=pool_w.astype(BF16),
                wa=w_branch_a.astype(BF16), wb=w_branch_b.astype(BF16), wo=w_out.astype(BF16),
                wq=peer_wq.astype(BF16), kt=kt, u_b=peer_u.astype(BF16), vt_b=peer_v.T.astype(BF16))


def kernel(x_prompt, x_sample, cache_k, cache_v, cache_kidx, state_pool, page_table, c_prompt, c_sample,
           w_ada, b_ada, w_in, pool_w, pool_scale, w_branch_a, w_branch_b, w_out, ln1_g, ln1_b,
           peer_wq, peer_sub_keys, peer_u, peer_v, ln2_g, ln2_b):
    depth = w_ada.shape[0]
    b, s, d = x_prompt.shape
    db, t_new, _ = x_sample.shape
    page, n_heads, hd = cache_k.shape[2:]
    idim = cache_kidx.shape[3]
    pool_hist, pw = state_pool.shape[2:]
    n_pages = page_table.shape[1]
    past = n_pages * page
    aw, iw = n_heads * hd, IDX_HEADS * idim
    n_keys = peer_sub_keys.shape[2]
    alpha = (2 * depth) ** 0.25
    idx_scale = float(iw) ** -0.5
    q_scale = float(hd) ** -0.5 * LOG2E
    assert t_new == SUBLANES and n_heads == SUBLANES and pool_hist < HIST_ROWS and s % TOKEN_TILE == 0

    n_c = b + db
    c_all = jnp.pad(jnp.concatenate([c_prompt, c_sample], axis=0), ((0, -n_c % SUBLANES), (0, 0)))
    y_p, y_s = x_prompt, x_sample
    k_t = jnp.transpose(cache_k, (0, 1, 3, 4, 2))
    v_t = jnp.transpose(cache_v, (0, 1, 3, 4, 2))
    ki_t = jnp.transpose(cache_kidx, (0, 1, 3, 2))
    outs = [[] for _ in range(8)]
    for l in range(depth):
        w = _layer_weights(w_in[l], pool_w[l], w_branch_a[l], w_branch_b[l], w_out[l], peer_wq[l], peer_sub_keys[l],
                           peer_u[l], peer_v[l], pw=pw, aw=aw, iw=iw, idim=idim)
        mod = _ada(c_all, w_ada[l], b_ada[l])
        mod_p, mod_s = mod[:, :b], mod[:, b:n_c]
        ps, g1, b1, g2, b2 = pool_scale[l][None], ln1_g[l][None], ln1_b[l][None], ln2_g[l][None], ln2_b[l][None]
        proj = functools.partial(_proj, wn=w["wn"], wki=w["wki"], wg=w["wg"], wt=w["wt"],
                                 pw=pw, aw=aw, iw=iw, idim=idim, q_scale=q_scale)

        x3, mod4, ks = _grouped(y_p, mod_p, TOKEN_TILE)
        u, k, v, ki, kb, kib, vt, qt, qit, wit, ga, gb = proj(x3, mod4, ks)
        nck = s // TOKEN_TILE
        a = _pool_prompt(u.reshape(b, s, pw), w["pool_w"], ps)
        o = _attn_prompt(qt.reshape(b, nck, aw, TOKEN_TILE), qit.reshape(b, nck, iw, TOKEN_TILE),
                         wit.reshape(b, nck, 2 * SUBLANES, TOKEN_TILE), kb.reshape(b, nck, TOKEN_TILE, aw),
                         vt.reshape(b, nck, aw, TOKEN_TILE), kib.reshape(b, nck, TOKEN_TILE, idim),
                         n_sel=min(TOPK_MAX, s // 4), idx_scale=idx_scale, n_heads=n_heads, hd=hd, idim=idim)
        x1 = _merge(x3, mod4, ks, a.reshape(b * s, pw), o.reshape(b * s, aw), ga, gb,
                    w["wa"], w["wb"], w["wo"], g1, b1, alpha)
        x3p, mod4p, ksp = _grouped(x1.reshape(b, s, d), mod_p, PEER_TOKEN_TILE)
        y_p = _peer(x3p, mod4p, ksp, w["wq"], w["kt"], w["u_b"], w["vt_b"], g2, b2, alpha, n_keys).reshape(b, s, d)
        outs[0].append(k.reshape(b, s, n_heads, hd))
        outs[1].append(v.reshape(b, s, n_heads, hd))
        outs[2].append(ki.reshape(b, s, idim))
        outs[3].append(u.reshape(b, s, pw)[:, s - pool_hist:])

        x3, mod4, ks = _grouped(y_s, mod_s, TOKEN_TILE)
        u, k, v, ki, kb, kib, vt, qt, qit, wit, ga, gb = proj(x3, mod4, ks)
        hist = state_pool[l]
        hist16 = jnp.pad(hist, ((0, 0), (HIST_ROWS - pool_hist, 0), (0, 0)))
        a = _pool_sample(u.reshape(db, t_new, pw), hist16, w["pool_w"], ps, past)
        untile = lambda z: jnp.moveaxis(z, 0, 1).reshape(z.shape[1], db, t_new)
        group = n_heads // 2
        q_s = jnp.transpose(untile(qt).reshape(2, group, hd, db, t_new), (3, 0, 1, 4, 2))
        qbd = jnp.einsum("bgltd,lm->bgltmd", q_s, jnp.eye(group, dtype=BF16)).reshape(db, 2, group * t_new, group * hd)
        qi2 = jnp.transpose(untile(qit).reshape(IDX_HEADS, idim, db, t_new), (2, 0, 3, 1)).reshape(db, IDX_HEADS * t_new, idim)
        wi3 = jnp.transpose(untile(wit)[:IDX_HEADS], (1, 0, 2))[..., None]
        lane_pad = lambda z: jnp.pad(z, ((0, 0),) * (z.ndim - 1) + ((0, page - t_new),))
        feat_major = lambda z: lane_pad(jnp.transpose(z.reshape(db, t_new, 2, group * hd), (0, 2, 3, 1)))
        knew = feat_major(kb)
        vnew = feat_major(v.astype(BF16))
        kinew = lane_pad(jnp.transpose(kib.reshape(db, t_new, idim), (0, 2, 1)))
        o = _attn_sample(page_table, qbd, qi2, wi3, knew, vnew, kinew, k_t, v_t, ki_t, l,
                         n_sel=min(TOPK_MAX, (past + t_new) // 4), idx_scale=idx_scale)
        o = jnp.transpose(o, (0, 2, 1, 3)).reshape(db * t_new, aw).astype(BF16)
        x1 = _merge(x3, mod4, ks, a, o, ga, gb, w["wa"], w["wb"], w["wo"], g1, b1, alpha)
        x3p, mod4p, ksp = _grouped(x1, mod_s, PEER_TOKEN_TILE)
        y_s = _peer(x3p, mod4p, ksp, w["wq"], w["kt"], w["u_b"], w["vt_b"], g2, b2, alpha, n_keys)
        outs[4].append(k.reshape(db, t_new, n_heads, hd))
        outs[5].append(v.reshape(db, t_new, n_heads, hd))
        outs[6].append(ki.reshape(db, t_new, idim))
        outs[7].append(jnp.concatenate([hist, u.reshape(db, t_new, pw)], axis=1)[:, t_new:])
    return (y_p, y_s) + tuple(jnp.stack(o_) for o_ in outs)
```

```python
import functools
import math

import jax
import jax.numpy as jnp
import numpy as np
from jax import lax
from jax.experimental import pallas as pl
from jax.experimental.pallas import tpu as pltpu

F32, BF16, I32 = jnp.float32, jnp.bfloat16, jnp.int32

POOL_WINDOWS = (2, 4, 8, 16)
IDX_HEADS = 8
TOPK_MAX = 256
PEER_HEADS = 8
PEER_TOPK = 16
LN_EPS = 1e-5
N_ADA = 6

VMEM_LIMIT_BYTES = 56 * 1024 * 1024
SUBLANES = 8
LANES = 128
BF16_ROWS = 16
TOKEN_TILE = 256
PEER_TOKEN_TILE = 512
PEER_EXPERT_TILE = 1024
PEER_TOKEN_SPLIT = 2
SAMPLE_SEARCH_BITS = 4
HIST_ROWS = 16

INT_MIN = -(2 ** 31)
KEY_NEG_INF = -2139095041
NEG_BIG = -1e30
LOG2E = 1.4426950408889634


def _cparams(sem):
    return pltpu.CompilerParams(dimension_semantics=sem, vmem_limit_bytes=VMEM_LIMIT_BYTES)


def _dot(a, b):
    return jnp.dot(a, b, preferred_element_type=F32)


def _dot_nt(a, b):
    return lax.dot_general(a, b, (((1,), (1,)), ((), ())), preferred_element_type=F32)


def _key_to_float(key):
    return lax.bitcast_convert_type(key ^ ((key >> 31) & 0x7FFFFFFF), F32)


def _threshold_step(i, key, count_ge, n_sel, bits=1):
    unit = lax.shift_left(jnp.int32(1), 32 - bits * (i + 1))
    digit = jnp.zeros_like(key)
    for j in range(1, 2 ** bits):
        trial = key + j * unit
        ok = jnp.logical_or(count_ge(_key_to_float(trial)) >= n_sel, trial < KEY_NEG_INF)
        digit = digit + jnp.where(ok, 1, 0)
    return key + digit * unit


def _ada_kernel(c_ref, w_ref, b_ref, o_ref):
    o_ref[...] = _dot(c_ref[...].astype(BF16), w_ref[...].astype(BF16)) + b_ref[...]


def _ada(c_all, w_ada, b_ada):
    n, d = c_all.shape
    return pl.pallas_call(
        _ada_kernel,
        grid=(N_ADA,),
        in_specs=[pl.BlockSpec((n, d), lambda j: (0, 0)),
                  pl.BlockSpec((d, d), lambda j: (0, j)),
                  pl.BlockSpec((1, d), lambda j: (0, j))],
        out_specs=pl.BlockSpec((None, n, d), lambda j: (j, 0, 0)),
        out_shape=jax.ShapeDtypeStruct((N_ADA, n, d), F32),
        compiler_params=_cparams(("arbitrary",)),
        name="ada",
    )(c_all, w_ada, b_ada.reshape(1, -1))


def _grouped(x, mod, rows):
    nseq, l, d = x.shape
    if l >= rows:
        assert l % rows == 0
        per = l // rows
        x3 = x.reshape(nseq * per, rows, d)
        mod4 = jnp.broadcast_to(mod[:, :, None, None, :], (N_ADA, nseq, per, 1, d)).reshape(N_ADA, nseq * per, 1, d)
        return x3, mod4, 1
    assert rows % l == 0 and nseq % (rows // l) == 0
    return x, mod[:, :, None, :], rows // l


def _proj_kernel(x_ref, mod_ref, wn_ref, wki_ref, wg_ref, wt_ref,
                 u_ref, kb_ref, kib_ref, kt32_ref, vt32_ref, kit32_ref, vt_ref, qt_ref, qit_ref, wit_ref, ga_ref, gb_ref,
                 *, pw, aw, iw, idim, q_scale):
    ks, r, d = x_ref.shape
    h = (x_ref[...] * (1.0 + mod_ref[1]) + mod_ref[0]).reshape(ks * r, d).astype(BF16)
    u_ref[...] = _dot(h, wn_ref[:, 0:pw])
    kb_ref[...] = _dot(h, wn_ref[:, pw:pw + aw]).astype(BF16)
    kib_ref[...] = _dot(h, wki_ref[...]).astype(BF16)
    ga_ref[...] = _dot(h, wg_ref[:, 0:d])
    gb_ref[...] = _dot(h, wg_ref[:, d:2 * d])
    rows = np.cumsum([0, aw, aw, iw, 2 * SUBLANES, aw, idim])
    part = lambda j: _dot_nt(wt_ref[rows[j]:rows[j + 1], :], h)
    vt = part(0)
    vt32_ref[...] = vt
    vt_ref[...] = vt.astype(BF16)
    qt_ref[...] = (part(1) * q_scale).astype(BF16)
    qit_ref[...] = part(2).astype(BF16)
    wit_ref[...] = part(3)
    kt32_ref[...] = part(4)
    kit32_ref[...] = part(5)


def _proj(x3, mod4, ks, wn, wki, wg, wt, *, pw, aw, iw, idim, q_scale, seq_chunks=None):
    g, r, d = x3.shape
    tm = ks * r
    nb = g // ks
    n = g * r
    row = lambda cols: pl.BlockSpec((tm, cols), lambda i: (i, 0))
    col = lambda rows: pl.BlockSpec((None, rows, tm), lambda i: (i, 0, 0))
    full = lambda a: pl.BlockSpec(a.shape, lambda i: (0,) * a.ndim)
    if seq_chunks is None:
        state = lambda rows: ((nb, rows, tm), F32, col(rows))
    else:
        state = lambda rows: ((nb // seq_chunks, rows, seq_chunks * tm), F32,
                              pl.BlockSpec((None, rows, tm), lambda i: (i // seq_chunks, 0, i % seq_chunks)))
    outs = [((n, pw), F32, row(pw)), ((n, aw), BF16, row(aw)), ((n, idim), BF16, row(idim)),
            state(aw), state(aw), state(idim),
            ((nb, aw, tm), BF16, col(aw)), ((nb, aw, tm), BF16, col(aw)), ((nb, iw, tm), BF16, col(iw)),
            ((nb, 2 * SUBLANES, tm), F32, col(2 * SUBLANES)),
            ((n, d), F32, row(d)), ((n, d), F32, row(d))]
    return pl.pallas_call(
        functools.partial(_proj_kernel, pw=pw, aw=aw, iw=iw, idim=idim, q_scale=q_scale),
        grid=(nb,),
        in_specs=[pl.BlockSpec((ks, r, d), lambda i: (i, 0, 0)),
                  pl.BlockSpec((N_ADA, ks, 1, d), lambda i: (0, i, 0, 0)),
                  full(wn), full(wki), full(wg), full(wt)],
        out_specs=[o[2] for o in outs],
        out_shape=[jax.ShapeDtypeStruct(o[0], o[1]) for o in outs],
        compiler_params=_cparams(("arbitrary",)),
        name="in_proj",
    )(x3, mod4, wn, wki, wg, wt)


def _pool_windows(ext_ref, pos, pw_ref, ps_ref, a_ref):
    ks, rows, width = ext_ref.shape
    r = rows - HIST_ROWS
    gw = width // len(POOL_WINDOWS)
    for g, w in enumerate(POOL_WINDOWS):
        lo = g * gw
        cur = ext_ref[:, HIST_ROWS:HIST_ROWS + r, lo:lo + gw]
        acc = cur
        for j in range(1, w):
            acc = acc + ext_ref[:, HIST_ROWS - j:HIST_ROWS - j + r, lo:lo + gw]
        cnt = jnp.minimum(pos + 1, w).astype(F32)
        pooled = (acc / cnt - cur).reshape(ks * r, gw).astype(BF16)
        mixed = _dot(pooled, pw_ref[g])
        a_ref[:, lo:lo + gw] = (mixed * ps_ref[:, lo:lo + gw]).astype(BF16)


def _pool_prompt_kernel(u_ref, pw_ref, ps_ref, a_ref, ext_ref):
    i = pl.program_id(1)
    tp, width = u_ref.shape
    gw = width // len(POOL_WINDOWS)

    @pl.when(i == 0)
    def _():
        ext_ref[:, 0:HIST_ROWS, :] = jnp.zeros((1, HIST_ROWS, width), F32)

    @pl.when(i > 0)
    def _():
        ext_ref[:, 0:HIST_ROWS, :] = ext_ref[:, tp:tp + HIST_ROWS, :]

    ext_ref[:, HIST_ROWS:, :] = u_ref[...][None]
    pos = i * tp + lax.broadcasted_iota(I32, (1, tp, gw), 1)
    _pool_windows(ext_ref, pos, pw_ref, ps_ref, a_ref)


def _pool_prompt(u, pool_w, pool_scale):
    b, s, width = u.shape
    tp = TOKEN_TILE
    return pl.pallas_call(
        _pool_prompt_kernel,
        grid=(b, s // tp),
        in_specs=[pl.BlockSpec((None, tp, width), lambda bi, i: (bi, i, 0)),
                  pl.BlockSpec(pool_w.shape, lambda bi, i: (0, 0, 0)),
                  pl.BlockSpec(pool_scale.shape, lambda bi, i: (0, 0))],
        out_specs=pl.BlockSpec((None, tp, width), lambda bi, i: (bi, i, 0)),
        out_shape=jax.ShapeDtypeStruct((b, s, width), BF16),
        scratch_shapes=[pltpu.VMEM((1, HIST_ROWS + tp, width), F32)],
        compiler_params=_cparams(("arbitrary", "arbitrary")),
        name="pool_prompt",
    )(u, pool_w, pool_scale)


def _pool_sample_kernel(u_ref, hist_ref, pw_ref, ps_ref, a_ref, ext_ref, *, pos0):
    ks, t, width = u_ref.shape
    gw = width // len(POOL_WINDOWS)
    ext_ref[:, 0:HIST_ROWS, :] = hist_ref[...]
    ext_ref[:, HIST_ROWS:, :] = u_ref[...]
    pos = pos0 + lax.broadcasted_iota(I32, (1, t, gw), 1)
    _pool_windows(ext_ref, pos, pw_ref, ps_ref, a_ref)


def _pool_sample(u, hist16, pool_w, pool_scale, pos0):
    db, t, width = u.shape
    ks = TOKEN_TILE // t
    return pl.pallas_call(
        functools.partial(_pool_sample_kernel, pos0=pos0),
        grid=(db // ks,),
        in_specs=[pl.BlockSpec((ks, t, width), lambda i: (i, 0, 0)),
                  pl.BlockSpec((ks, HIST_ROWS, width), lambda i: (i, 0, 0)),
                  pl.BlockSpec(pool_w.shape, lambda i: (0, 0, 0)),
                  pl.BlockSpec(pool_scale.shape, lambda i: (0, 0))],
        out_specs=pl.BlockSpec((ks * t, width), lambda i: (i, 0)),
        out_shape=jax.ShapeDtypeStruct((db * t, width), BF16),
        scratch_shapes=[pltpu.VMEM((ks, HIST_ROWS + t, width), F32)],
        compiler_params=_cparams(("arbitrary",)),
        name="pool_sample",
    )(u, hist16, pool_w, pool_scale)


def _alibi_slopes(n_heads):
    return [2.0 ** (-8.0 * (h + 1) / n_heads) for h in range(n_heads)]


def _attn_prompt_kernel(qt_ref, qit_ref, wit_ref, kb_ref, vt_ref, kib_ref, o_ref,
                        sc_sc, qz_sc, acc_sc, lg_sc, al_sc, *, n_sel, idx_scale, n_heads, hd, idim, seq_len):
    qb = pl.program_id(1)
    tq = qt_ref.shape[-1]
    sk = tq
    nchunk = qb + 1
    t_idx = qb * tq + lax.broadcasted_iota(I32, (sk, tq), 1)
    s_loc = lax.broadcasted_iota(I32, (sk, tq), 0)

    def score_body(c, carry):
        kic = kib_ref[c]
        acc = jnp.zeros((sk, tq), F32)
        for h in range(IDX_HEADS):
            rel = _dot(kic, qit_ref[h * idim:(h + 1) * idim, :])
            acc = acc + wit_ref[h:h + 1, :] * jnp.maximum(rel, 0.0)
        sc_sc[c] = jnp.where(c * sk + s_loc <= t_idx, acc * idx_scale, -jnp.inf)
        return carry

    lax.fori_loop(0, nchunk, score_body, 0)

    def colsum(mask):
        return jnp.where(mask, 1.0, 0.0).reshape(sk // SUBLANES, SUBLANES, tq).sum(axis=0)

    @pl.when(nchunk % 2 == 1)
    def _():
        sc_sc[nchunk] = jnp.full((sk, tq), -jnp.inf, F32)

    def count(pred):
        def pair(c2, a):
            c = 2 * c2
            return a + colsum(pred(c, sc_sc[c])) + colsum(pred(c + 1, sc_sc[c + 1]))
        acc = lax.fori_loop(0, (nchunk + 1) // 2, pair, jnp.zeros((SUBLANES, tq), F32))
        return acc.sum(axis=0, keepdims=True)

    count_ge = lambda t: count(lambda c, s: s >= t)
    tau_key = lax.fori_loop(0, 32, lambda i, k: _threshold_step(i, k, count_ge, n_sel),
                            jnp.full((1, tq), INT_MIN, I32))
    tau = _key_to_float(tau_key)

    tie_rows = jnp.logical_and(count_ge(tau) > n_sel, tau_key > KEY_NEG_INF)
    has_tie = jnp.max(jnp.where(tie_rows, 1.0, 0.0)) > 0.5

    @pl.when(has_tie)
    def _():
        need = n_sel - count(lambda c, s: s > tau)
        nbits = max(1, int(math.ceil(math.log2(seq_len))))

        def first_enough(i, j):
            step = lax.shift_left(jnp.int32(1), nbits - 1 - i)
            upto = count(lambda c, s: jnp.logical_and(s == tau, c * sk + s_loc <= j + step - 1))
            return jnp.where(upto < need, j + step, j)

        last = lax.fori_loop(0, nbits, first_enough, jnp.zeros((1, tq), I32))

        def demote(c, carry):
            s = sc_sc[c]
            drop = jnp.logical_and(jnp.logical_and(tie_rows, s == tau), c * sk + s_loc > last)
            sc_sc[c] = jnp.where(drop, -jnp.inf, s)
            return carry

        lax.fori_loop(0, nchunk, demote, 0)

    pair_rows = lax.broadcasted_iota(I32, (2 * hd, tq), 0)
    for h in range(n_heads):
        blk = qt_ref[(h // 2) * 2 * hd:(h // 2 + 1) * 2 * hd, :]
        keep = pair_rows < hd if h % 2 == 0 else pair_rows >= hd
        qz_sc[h] = jnp.where(keep, blk, jnp.zeros_like(blk))
    acc_sc[...] = jnp.zeros(acc_sc.shape, F32)
    slopes = [s * LOG2E for s in _alibi_slopes(n_heads)]
    for h in range(n_heads):
        al_sc[h] = slopes[h] * s_loc.astype(F32)

    def att_chunk(c, j, m_all, l_all):
        sel = jnp.logical_and(sc_sc[c] >= tau, c * sk + s_loc <= t_idx)
        off = ((c - qb) * sk).astype(F32)
        m_out, l_out = [], []
        for h in range(n_heads):
            lg = jnp.where(sel, lg_sc[j, h] + al_sc[h], -jnp.inf)
            m_new = jnp.maximum(m_all[h], jnp.max(lg, axis=0, keepdims=True) + slopes[h] * off)
            alpha = jnp.exp2(m_all[h] - m_new)
            p = jnp.exp2(lg - (m_new - slopes[h] * off))
            l_out.append(alpha * l_all[h] + jnp.sum(p, axis=0, keepdims=True))
            m_out.append(m_new)
            pv = _dot(vt_ref[c, h * hd:(h + 1) * hd, :], p.astype(BF16))
            acc_sc[h * hd:(h + 1) * hd, :] = alpha * acc_sc[h * hd:(h + 1) * hd, :] + pv
        return tuple(m_out), tuple(l_out)

    def att_pair(c2, carry):
        for j in range(2):
            for h in range(n_heads):
                lg_sc[j, h] = _dot(kb_ref[2 * c2 + j, :, (h // 2) * 2 * hd:(h // 2 + 1) * 2 * hd], qz_sc[h])
        for j in range(2):
            carry = att_chunk(2 * c2 + j, j, *carry)
        return carry

    init = (tuple(jnp.full((1, tq), NEG_BIG, F32) for _ in range(n_heads)),
            tuple(jnp.zeros((1, tq), F32) for _ in range(n_heads)))
    _, l_all = lax.fori_loop(0, (nchunk + 1) // 2, att_pair, init)
    out_t = jnp.concatenate([acc_sc[h * hd:(h + 1) * hd, :] / l_all[h] for h in range(n_heads)], axis=0)
    o_ref[...] = out_t.T.astype(BF16)


def _attn_prompt(qt, qit, wit, kb, vt, kib, *, n_sel, idx_scale, n_heads, hd, idim):
    b, nck, aw, tq = qt.shape
    assert nck % 2 == 0
    iw = qit.shape[2]
    resident = lambda shape: pl.BlockSpec((None,) + shape, lambda bi, i: (bi, 0, 0, 0), pipeline_mode=pl.Buffered(1))
    tile = lambda rows: pl.BlockSpec((None, None, rows, tq), lambda bi, i: (bi, i, 0, 0))
    return pl.pallas_call(
        functools.partial(_attn_prompt_kernel, n_sel=n_sel, idx_scale=idx_scale, n_heads=n_heads, hd=hd, idim=idim,
                          seq_len=nck * tq),
        grid=(b, nck),
        in_specs=[tile(aw), tile(iw), tile(2 * SUBLANES),
                  resident((nck, tq, aw)), resident((nck, aw, tq)), resident((nck, tq, idim))],
        out_specs=pl.BlockSpec((None, tq, aw), lambda bi, i: (bi, i, 0)),
        out_shape=jax.ShapeDtypeStruct((b, nck * tq, aw), BF16),
        scratch_shapes=[pltpu.VMEM((nck, tq, tq), F32),
                        pltpu.VMEM((n_heads, 2 * hd, tq), BF16),
                        pltpu.VMEM((aw, tq), F32),
                        pltpu.VMEM((2, n_heads, tq, tq), F32),
                        pltpu.VMEM((n_heads, tq, tq), F32)],
        compiler_params=_cparams(("arbitrary", "arbitrary")),
        name="attn_prompt",
    )(qt, qit, wit, kb, vt, kib)


def _attn_sample_kernel(pt_ref, qbd_ref, qi_ref, wi_ref, knew_ref, vnew_ref, kinew_ref, *rest,
                        n_pages, page, n_sel, idx_scale, n_heads, hd, t_new):
    del pt_ref
    ki_refs, k_refs, v_refs = rest[0:n_pages], rest[n_pages:2 * n_pages], rest[2 * n_pages:3 * n_pages]
    o_ref, sc_sc, lg_sc = rest[3 * n_pages:]
    past = n_pages * page
    width = past + page
    group = n_heads // 2
    row = lax.broadcasted_iota(I32, (t_new, page), 0)
    lane = lax.broadcasted_iota(I32, (t_new, page), 1)
    wi = wi_ref[...]

    def scores(kit):
        rel = jnp.maximum(_dot(qi_ref[...], kit), 0.0)
        return jnp.sum(rel.reshape(IDX_HEADS, t_new, rel.shape[1]) * wi, axis=0) * idx_scale

    def page_pair(refs, pp, head_lo=None):
        if head_lo is None:
            parts = [refs[2 * pp + j][...] for j in range(2)]
        else:
            parts = [refs[2 * pp + j][head_lo:head_lo + group].reshape(group * hd, page) for j in range(2)]
        return jnp.concatenate(parts, axis=1).astype(BF16)

    for pp in range(n_pages // 2):
        sc_sc[:, 2 * pp * page:(2 * pp + 2) * page] = scores(page_pair(ki_refs, pp))
    sc_sc[:, past:width] = jnp.where(lane <= row, scores(kinew_ref[...]), -jnp.inf)

    def count(pred):
        return jnp.sum(jnp.where(pred(sc_sc[...]), 1.0, 0.0), axis=1, keepdims=True)

    count_ge = lambda t: count(lambda s: s >= t)
    tau_key = lax.fori_loop(0, 32 // SAMPLE_SEARCH_BITS,
                            lambda i, k: _threshold_step(i, k, count_ge, n_sel, SAMPLE_SEARCH_BITS),
                            jnp.full((t_new, 1), INT_MIN, I32))
    tau = _key_to_float(tau_key)

    tie_rows = jnp.logical_and(count_ge(tau) > n_sel, tau_key > KEY_NEG_INF)
    has_tie = jnp.max(jnp.where(tie_rows, 1.0, 0.0)) > 0.5
    pos = lax.broadcasted_iota(I32, (t_new, width), 1)

    @pl.when(has_tie)
    def _():
        need = n_sel - count(lambda s: s > tau)
        nbits = max(1, int(math.ceil(math.log2(width))))

        def first_enough(i, j):
            step = lax.shift_left(jnp.int32(1), nbits - 1 - i)
            upto = count(lambda s: jnp.logical_and(s == tau, pos <= j + step - 1))
            return jnp.where(upto < need, j + step, j)

        last = lax.fori_loop(0, nbits, first_enough, jnp.zeros((t_new, 1), I32))
        s = sc_sc[...]
        drop = jnp.logical_and(jnp.logical_and(tie_rows, s == tau), pos > last)
        sc_sc[...] = jnp.where(drop, -jnp.inf, s)

    slopes = [s * LOG2E for s in _alibi_slopes(n_heads)]
    t_pos = past + lax.broadcasted_iota(I32, (t_new, width), 0)
    sel = jnp.logical_and(sc_sc[...] >= tau, pos <= t_pos)
    bias = jnp.where(sel, 0.0, -jnp.inf)
    dist = (t_pos - pos).astype(F32)
    spans = [(2 * pp * page, 2 * page) for pp in range(n_pages // 2)] + [(past, page)]
    for g in range(2):
        for pp, (lo, n) in enumerate(spans):
            kt = page_pair(k_refs, pp, g * group) if pp < n_pages // 2 else knew_ref[g]
            lg = _dot(qbd_ref[g], kt).reshape(group, t_new, n)
            for hl in range(group):
                h = g * group + hl
                lg_sc[h, :, lo:lo + n] = lg[hl] - slopes[h] * dist[:, lo:lo + n] + bias[:, lo:lo + n]
    for g in range(2):
        probs, denoms = [], []
        for hl in range(group):
            lg = lg_sc[g * group + hl]
            pr = jnp.exp2(lg - jnp.max(lg, axis=1, keepdims=True))
            denoms.append(jnp.sum(pr, axis=1, keepdims=True))
            probs.append(pr)
        pg = jnp.concatenate(probs, axis=0).astype(BF16)
        acc = jnp.zeros((group * t_new, group * hd), F32)
        for pp, (lo, n) in enumerate(spans):
            vt = page_pair(v_refs, pp, g * group) if pp < n_pages // 2 else vnew_ref[g]
            acc = acc + _dot_nt(pg[:, lo:lo + n], vt)
        for hl in range(group):
            o_ref[g * group + hl] = acc[hl * t_new:(hl + 1) * t_new, hl * hd:(hl + 1) * hd] / denoms[hl]


def _attn_sample(page_table, qbd, qi2, wi3, knew, vnew, kinew, k_t, v_t, ki_t, layer, *, n_sel, idx_scale):
    db = qbd.shape[0]
    n_pages = page_table.shape[1]
    n_heads, hd, page = k_t.shape[2:]
    idim = ki_t.shape[2]
    t_new = wi3.shape[2]
    assert n_pages % 2 == 0 and n_heads % 2 == 0
    per_seq = lambda a: pl.BlockSpec((None,) + a.shape[1:], lambda b, pt: (b,) + (0,) * (a.ndim - 1))
    kv_page = lambda j: pl.BlockSpec((None, None, n_heads, hd, page), lambda b, pt: (layer, pt[b, j], 0, 0, 0))
    ki_page = lambda j: pl.BlockSpec((None, None, idim, page), lambda b, pt: (layer, pt[b, j], 0, 0))
    in_specs = ([per_seq(qbd), per_seq(qi2), per_seq(wi3), per_seq(knew), per_seq(vnew), per_seq(kinew)]
                + [ki_page(j) for j in range(n_pages)]
                + [kv_page(j) for j in range(n_pages)]
                + [kv_page(j) for j in range(n_pages)])
    grid_spec = pltpu.PrefetchScalarGridSpec(
        num_scalar_prefetch=1,
        grid=(db,),
        in_specs=in_specs,
        out_specs=pl.BlockSpec((None, n_heads, t_new, hd), lambda b, pt: (b, 0, 0, 0)),
        scratch_shapes=[pltpu.VMEM((t_new, (n_pages + 1) * page), F32),
                        pltpu.VMEM((n_heads, t_new, (n_pages + 1) * page), F32)],
    )
    return pl.pallas_call(
        functools.partial(_attn_sample_kernel, n_pages=n_pages, page=page, n_sel=n_sel, idx_scale=idx_scale,
                          n_heads=n_heads, hd=hd, t_new=t_new),
        grid_spec=grid_spec,
        out_shape=jax.ShapeDtypeStruct((db, n_heads, t_new, hd), F32),
        compiler_params=_cparams(("arbitrary",)),
        name="attn_sample",
    )(page_table, qbd, qi2, wi3, knew, vnew, kinew, *([ki_t] * n_pages), *([k_t] * n_pages), *([v_t] * n_pages))


def _layer_norm(y, g, b):
    mu = jnp.mean(y, axis=-1, keepdims=True)
    var = jnp.mean(jnp.square(y - mu), axis=-1, keepdims=True)
    return (y - mu) * lax.rsqrt(var + LN_EPS) * g + b


def _merge_kernel(x_ref, mod_ref, a_ref, o_ref, ga_ref, gb_ref, wa_ref, wb_ref, wo_ref, g_ref, b_ref, x1_ref, *, alpha):
    ks, r, d = x_ref.shape
    merged = (jax.nn.sigmoid(ga_ref[...]) * _dot(a_ref[...], wa_ref[...])
              + jax.nn.sigmoid(gb_ref[...]) * _dot(o_ref[...], wb_ref[...]))
    upd = _dot(merged.astype(BF16), wo_ref[...]).reshape(ks, r, d)
    y = alpha * x_ref[...] + mod_ref[2] * upd
    x1_ref[...] = _layer_norm(y, g_ref[...], b_ref[...])


def _merge(x3, mod4, ks, a, o, ga, gb, wa, wb, wo, ln_g, ln_b, alpha):
    g, r, d = x3.shape
    tm = ks * r
    row = lambda arr: pl.BlockSpec((tm, arr.shape[1]), lambda i: (i, 0))
    full = lambda arr: pl.BlockSpec(arr.shape, lambda i: (0,) * arr.ndim)
    return pl.pallas_call(
        functools.partial(_merge_kernel, alpha=alpha),
        grid=(g // ks,),
        in_specs=[pl.BlockSpec((ks, r, d), lambda i: (i, 0, 0)),
                  pl.BlockSpec((N_ADA, ks, 1, d), lambda i: (0, i, 0, 0)),
                  row(a), row(o), row(ga), row(gb), full(wa), full(wb), full(wo), full(ln_g), full(ln_b)],
        out_specs=pl.BlockSpec((ks, r, d), lambda i: (i, 0, 0)),
        out_shape=jax.ShapeDtypeStruct((g, r, d), F32),
        compiler_params=_cparams(("arbitrary",)),
        name="merge_ln1",
    )(x3, mod4, a, o, ga, gb, wa, wb, wo, ln_g, ln_b)


def _top_rows_exact(s, n_top):
    rows = s.shape[0]
    row_id = lax.broadcasted_iota(I32, s.shape, 0)
    rank = jnp.full(s.shape, n_top, I32)
    vals = []
    for r in range(n_top):
        m = jnp.max(s, axis=0, keepdims=True)
        idx = jnp.min(jnp.where(s == m, row_id, rows), axis=0, keepdims=True)
        hit = row_id == idx
        vals.append(m)
        rank = jnp.where(hit, r, rank)
        s = jnp.where(hit, -jnp.inf, s)
    return jnp.concatenate(vals, axis=0), rank


def _top_rows_distinct(s, n_top):
    rank = jnp.full(s.shape, n_top, I32)
    vals = []
    for r in range(n_top):
        m = jnp.max(s, axis=0, keepdims=True)
        hit = s == m
        vals.append(m)
        rank = jnp.where(hit, r, rank)
        s = jnp.where(hit, -jnp.inf, s)
    return jnp.concatenate(vals, axis=0), rank


def _rank_ties(rank, n_top):
    taken = jnp.sum(jnp.where(rank < n_top, 1.0, 0.0), axis=0, keepdims=True)
    return jnp.max(jnp.where(taken != n_top, 1.0, 0.0)) > 0.5


PEER_PAIRS = [(a, b) for a in range(PEER_TOPK) for b in range(PEER_TOPK) if (a + 1) * (b + 1) <= PEER_TOPK]
PEER_CAND_ROWS = -(-len(PEER_PAIRS) // SUBLANES) * SUBLANES


def _peer_kernel(x_ref, mod_ref, wq_ref, kt_ref, u_ref, vt_ref, g_ref, b_ref, y_ref,
                 h2_sc, st_sc, e0_sc, nb_sc, e1_sc, r1_sc, acc_sc, sv_sc, rk_sc, ts_sc, r2_sc, act_sc, w_sc,
                 *, alpha, n_keys):
    ks, r, d = x_ref.shape
    tm = ks * r
    step = pl.program_id(1)
    sub = PEER_EXPERT_TILE // n_keys

    @pl.when(step == 0)
    def _():
        h2 = (x_ref[...] * (1.0 + mod_ref[4]) + mod_ref[3]).reshape(tm, d).astype(BF16)
        h2_sc[...] = h2
        qp = _dot(h2, wq_ref[...]).astype(BF16)
        st_sc[...] = _dot_nt(kt_ref[...], qp)
        acc_sc[...] = jnp.zeros(acc_sc.shape, F32)

        def route(h, carry):
            s0 = st_sc[pl.ds(pl.multiple_of(h * 2 * n_keys, 2 * n_keys), n_keys), :]
            s1 = st_sc[pl.ds(pl.multiple_of(h * 2 * n_keys + n_keys, n_keys), n_keys), :]

            def halves(top):
                sv_sc[0], rk_sc[0] = top(s0, PEER_TOPK)
                sv_sc[1], rk_sc[1] = top(s1, PEER_TOPK)

            halves(_top_rows_distinct)

            @pl.when(jnp.logical_or(_rank_ties(rk_sc[0], PEER_TOPK), _rank_ties(rk_sc[1], PEER_TOPK)))
            def _():
                halves(_top_rows_exact)

            v0, v1, rank0 = sv_sc[0], sv_sc[1], rk_sc[0]
            pad = [jnp.full((PEER_CAND_ROWS - len(PEER_PAIRS), tm), -jnp.inf, F32)]
            cand = jnp.concatenate([v0[a:a + 1] + v1[b:b + 1] for a, b in PEER_PAIRS] + pad, axis=0)

            def pairs(top):
                ts_sc[...], r2_sc[...] = top(cand, PEER_TOPK)

            pairs(_top_rows_distinct)

            @pl.when(_rank_ties(r2_sc[...], PEER_TOPK))
            def _():
                pairs(_top_rows_exact)

            top_s, rank2 = ts_sc[...], r2_sc[...]
            z = jnp.sum(jnp.exp(top_s - top_s[0:1]), axis=0, keepdims=True)
            took = jnp.where(rank2 < PEER_TOPK, 1, 0)
            take = jnp.zeros(rank0.shape, I32)
            for a in range(PEER_TOPK):
                n_a = sum(took[i:i + 1] for i, (pa, _) in enumerate(PEER_PAIRS) if pa == a)
                take = jnp.where(rank0 == a, n_a, take)
            e0_sc[h] = (jnp.exp(s0 - v0[0:1]) / z * 0.5).reshape(n_keys // SUBLANES, SUBLANES, tm)
            nb_sc[h] = take.astype(F32).reshape(n_keys // SUBLANES, SUBLANES, tm)
            e1_sc[h] = jnp.exp(s1 - v1[0:1]).astype(BF16).reshape(n_keys // BF16_ROWS, BF16_ROWS, tm)
            r1_sc[h] = rk_sc[1].astype(F32).astype(BF16).reshape(n_keys // BF16_ROWS, BF16_ROWS, tm)
            return carry

        lax.fori_loop(0, PEER_HEADS, route, 0)

    n_tok = tm // PEER_TOKEN_SPLIT
    for t in range(PEER_TOKEN_SPLIT):
        act_sc[t] = _dot_nt(u_ref[...], h2_sc[t * n_tok:(t + 1) * n_tok, :])
    for t in range(PEER_TOKEN_SPLIT):
        lanes = slice(t * n_tok, (t + 1) * n_tok)
        for ii in range(sub):
            act = act_sc[t, ii * n_keys:(ii + 1) * n_keys, :]
            gate = jnp.zeros((n_keys // BF16_ROWS, BF16_ROWS, n_tok), BF16)
            for h in range(PEER_HEADS):
                take_i = jnp.broadcast_to(nb_sc[h, step, ii:ii + 1, lanes], (BF16_ROWS, n_tok)).astype(BF16)
                e0_i = jnp.broadcast_to(e0_sc[h, step, ii:ii + 1, lanes], (BF16_ROWS, n_tok)).astype(BF16)
                e1 = e1_sc[h, :, :, lanes]
                gate = gate + jnp.where(r1_sc[h, :, :, lanes] < take_i, e1, jnp.zeros_like(e1)) * e0_i
            gelu2 = act * (1.0 + lax.erf(act * (2.0 ** -0.5)))
            w = gelu2.astype(BF16).reshape(n_keys // BF16_ROWS, BF16_ROWS, n_tok) * gate
            w_sc[t, ii * n_keys:(ii + 1) * n_keys, :] = w.reshape(n_keys, n_tok)
    for t in range(PEER_TOKEN_SPLIT):
        acc_sc[:, t * n_tok:(t + 1) * n_tok] += _dot(vt_ref[...], w_sc[t])

    @pl.when(step == pl.num_programs(1) - 1)
    def _():
        f = acc_sc[...].T.reshape(ks, r, d)
        y = alpha * x_ref[...] + mod_ref[5] * f
        y_ref[...] = _layer_norm(y, g_ref[...], b_ref[...])


def _peer(x3, mod4, ks, wq, kt, u_b, vt_b, ln_g, ln_b, alpha, n_keys):
    g, r, d = x3.shape
    tm = ks * r
    n_exp = u_b.shape[0]
    assert PEER_EXPERT_TILE == SUBLANES * n_keys and n_exp % PEER_EXPERT_TILE == 0
    full = lambda arr: pl.BlockSpec(arr.shape, lambda i, e: (0,) * arr.ndim)
    return pl.pallas_call(
        functools.partial(_peer_kernel, alpha=alpha, n_keys=n_keys),
        grid=(g // ks, n_exp // PEER_EXPERT_TILE),
        in_specs=[pl.BlockSpec((ks, r, d), lambda i, e: (i, 0, 0)),
                  pl.BlockSpec((N_ADA, ks, 1, d), lambda i, e: (0, i, 0, 0)),
                  full(wq), full(kt),
                  pl.BlockSpec((PEER_EXPERT_TILE, d), lambda i, e: (e, 0)),
                  pl.BlockSpec((d, PEER_EXPERT_TILE), lambda i, e: (0, e)),
                  full(ln_g), full(ln_b)],
        out_specs=pl.BlockSpec((ks, r, d), lambda i, e: (i, 0, 0)),
        out_shape=jax.ShapeDtypeStruct((g, r, d), F32),
        scratch_shapes=[pltpu.VMEM((tm, d), BF16),
                        pltpu.VMEM((kt.shape[0], tm), F32),
                        pltpu.VMEM((PEER_HEADS, n_keys // SUBLANES, SUBLANES, tm), F32),
                        pltpu.VMEM((PEER_HEADS, n_keys // SUBLANES, SUBLANES, tm), F32),
                        pltpu.VMEM((PEER_HEADS, n_keys // BF16_ROWS, BF16_ROWS, tm), BF16),
                        pltpu.VMEM((PEER_HEADS, n_keys // BF16_ROWS, BF16_ROWS, tm), BF16),
                        pltpu.VMEM((d, tm), F32),
                        pltpu.VMEM((2, PEER_TOPK, tm), F32),
                        pltpu.VMEM((2, n_keys, tm), I32),
                        pltpu.VMEM((PEER_TOPK, tm), F32),
                        pltpu.VMEM((PEER_CAND_ROWS, tm), I32),
                        pltpu.VMEM((PEER_TOKEN_SPLIT, PEER_EXPERT_TILE, tm // PEER_TOKEN_SPLIT), F32),
                        pltpu.VMEM((PEER_TOKEN_SPLIT, PEER_EXPERT_TILE, tm // PEER_TOKEN_SPLIT), BF16)],
        compiler_params=_cparams(("arbitrary", "arbitrary")),
        name="peer_ln2",
    )(x3, mod4, wq, kt, u_b, vt_b, ln_g, ln_b)


def _layer_weights(w_in, pool_w, w_branch_a, w_branch_b, w_out, peer_wq, peer_sub_keys, peer_u, peer_v,
                   *, pw, aw, iw, idim):
    d = w_in.shape[0]
    o = np.cumsum([0, pw, aw, aw, aw, iw, idim, IDX_HEADS, d, d])
    cut = lambda j: w_in[:, o[j]:o[j + 1]]
    u_w, q_w, k_w, v_w, qi_w, ki_w, wi_w, ga_w, gb_w = (cut(j) for j in range(9))
    wn = jnp.concatenate([u_w, k_w], axis=1).astype(BF16)
    wg = jnp.concatenate([ga_w, gb_w], axis=1).astype(BF16)
    wi_pad = jnp.pad(wi_w, ((0, 0), (0, 2 * SUBLANES - IDX_HEADS)))
    wt = jnp.concatenate([v_w, q_w, qi_w, wi_pad, k_w, ki_w], axis=1).T.astype(BF16)
    n_keys, half = peer_sub_keys.shape[1], peer_sub_keys.shape[2]
    kt = jnp.einsum("hg,cb,ckd->hckgbd", jnp.eye(PEER_HEADS, dtype=F32), jnp.eye(2, dtype=F32), peer_sub_keys)
    kt = kt.reshape(PEER_HEADS * 2 * n_keys, PEER_HEADS * 2 * half).astype(BF16)
    return dict(wn=wn, wki=ki_w.astype(BF16), wg=wg, wt=wt, pool_w=pool_w.astype(BF16),
                wa=w_branch_a.astype(BF16), wb=w_branch_b.astype(BF16), wo=w_out.astype(BF16),
                wq=peer_wq.astype(BF16), kt=kt, u_b=peer_u.astype(BF16), vt_b=peer_v.T.astype(BF16))


def kernel(x_prompt, x_sample, cache_k, cache_v, cache_kidx, state_pool, page_table, c_prompt, c_sample,
           w_ada, b_ada, w_in, pool_w, pool_scale, w_branch_a, w_branch_b, w_out, ln1_g, ln1_b,
           peer_wq, peer_sub_keys, peer_u, peer_v, ln2_g, ln2_b):
    depth = w_ada.shape[0]
    b, s, d = x_prompt.shape
    db, t_new, _ = x_sample.shape
    page, n_heads, hd = cache_k.shape[2:]
    idim = cache_kidx.shape[3]
    pool_hist, pw = state_pool.shape[2:]
    n_pages = page_table.shape[1]
    past = n_pages * page
    aw, iw = n_heads * hd, IDX_HEADS * idim
    n_keys = peer_sub_keys.shape[2]
    alpha = (2 * depth) ** 0.25
    idx_scale = float(iw) ** -0.5
    q_scale = float(hd) ** -0.5 * LOG2E
    assert t_new == SUBLANES and n_heads == SUBLANES and pool_hist < HIST_ROWS and s % TOKEN_TILE == 0

    n_c = b + db
    c_all = jnp.pad(jnp.concatenate([c_prompt, c_sample], axis=0), ((0, -n_c % SUBLANES), (0, 0)))
    y_p, y_s = x_prompt, x_sample
    k_t = jnp.transpose(cache_k, (0, 1, 3, 4, 2))
    v_t = jnp.transpose(cache_v, (0, 1, 3, 4, 2))
    ki_t = jnp.transpose(cache_kidx, (0, 1, 3, 2))
    outs = [[] for _ in range(8)]
    for l in range(depth):
        w = _layer_weights(w_in[l], pool_w[l], w_branch_a[l], w_branch_b[l], w_out[l], peer_wq[l], peer_sub_keys[l],
                           peer_u[l], peer_v[l], pw=pw, aw=aw, iw=iw, idim=idim)
        mod = _ada(c_all, w_ada[l], b_ada[l])
        mod_p, mod_s = mod[:, :b], mod[:, b:n_c]
        ps, g1, b1, g2, b2 = pool_scale[l][None], ln1_g[l][None], ln1_b[l][None], ln2_g[l][None], ln2_b[l][None]
        proj = functools.partial(_proj, wn=w["wn"], wki=w["wki"], wg=w["wg"], wt=w["wt"],
                                 pw=pw, aw=aw, iw=iw, idim=idim, q_scale=q_scale)

        x3, mod4, ks = _grouped(y_p, mod_p, TOKEN_TILE)
        nck = s // TOKEN_TILE
        u, kb, kib, kt32, vt32, kit32, vt, qt, qit, wit, ga, gb = proj(x3, mod4, ks, seq_chunks=nck)
        a = _pool_prompt(u.reshape(b, s, pw), w["pool_w"], ps)
        o = _attn_prompt(qt.reshape(b, nck, aw, TOKEN_TILE), qit.reshape(b, nck, iw, TOKEN_TILE),
                         wit.reshape(b, nck, 2 * SUBLANES, TOKEN_TILE), kb.reshape(b, nck, TOKEN_TILE, aw),
                         vt.reshape(b, nck, aw, TOKEN_TILE), kib.reshape(b, nck, TOKEN_TILE, idim),
                         n_sel=min(TOPK_MAX, s // 4), idx_scale=idx_scale, n_heads=n_heads, hd=hd, idim=idim)
        x1 = _merge(x3, mod4, ks, a.reshape(b * s, pw), o.reshape(b * s, aw), ga, gb,
                    w["wa"], w["wb"], w["wo"], g1, b1, alpha)
        x3p, mod4p, ksp = _grouped(x1.reshape(b, s, d), mod_p, PEER_TOKEN_TILE)
        y_p = _peer(x3p, mod4p, ksp, w["wq"], w["kt"], w["u_b"], w["vt_b"], g2, b2, alpha, n_keys).reshape(b, s, d)
        heads_last = lambda z: jnp.transpose(z.reshape(z.shape[0], n_heads, hd, z.shape[2]), (0, 3, 1, 2))
        outs[0].append(heads_last(kt32))
        outs[1].append(heads_last(vt32))
        outs[2].append(jnp.transpose(kit32, (0, 2, 1)))
        outs[3].append(u.reshape(b, s, pw)[:, s - pool_hist:])

        x3, mod4, ks = _grouped(y_s, mod_s, TOKEN_TILE)
        u, kb, kib, kt32, vt32, kit32, vt, qt, qit, wit, ga, gb = proj(x3, mod4, ks)
        hist = state_pool[l]
        hist16 = jnp.pad(hist, ((0, 0), (HIST_ROWS - pool_hist, 0), (0, 0)))
        a = _pool_sample(u.reshape(db, t_new, pw), hist16, w["pool_w"], ps, past)
        untile = lambda z: jnp.moveaxis(z, 0, 1).reshape(z.shape[1], db, t_new)
        group = n_heads // 2
        q_s = jnp.transpose(untile(qt).reshape(2, group, hd, db, t_new), (3, 0, 1, 4, 2))
        qbd = jnp.einsum("bgltd,lm->bgltmd", q_s, jnp.eye(group, dtype=BF16)).reshape(db, 2, group * t_new, group * hd)
        qi2 = jnp.transpose(untile(qit).reshape(IDX_HEADS, idim, db, t_new), (2, 0, 3, 1)).reshape(db, IDX_HEADS * t_new, idim)
        wi3 = jnp.transpose(untile(wit)[:IDX_HEADS], (1, 0, 2))[..., None]
        lane_pad = lambda z: jnp.pad(z, ((0, 0),) * (z.ndim - 1) + ((0, page - t_new),))
        knew = lane_pad(jnp.transpose(kb.reshape(db, t_new, 2, group * hd), (0, 2, 3, 1)))
        vnew = lane_pad(jnp.transpose(untile(vt).reshape(2, group * hd, db, t_new), (2, 0, 1, 3)))
        kinew = lane_pad(jnp.transpose(kib.reshape(db, t_new, idim), (0, 2, 1)))
        o = _attn_sample(page_table, qbd, qi2, wi3, knew, vnew, kinew, k_t, v_t, ki_t, l,
                         n_sel=min(TOPK_MAX, (past + t_new) // 4), idx_scale=idx_scale)
        o = jnp.transpose(o, (0, 2, 1, 3)).reshape(db * t_new, aw).astype(BF16)
        x1 = _merge(x3, mod4, ks, a, o, ga, gb, w["wa"], w["wb"], w["wo"], g1, b1, alpha)
        x3p, mod4p, ksp = _grouped(x1, mod_s, PEER_TOKEN_TILE)
        y_s = _peer(x3p, mod4p, ksp, w["wq"], w["kt"], w["u_b"], w["vt_b"], g2, b2, alpha, n_keys)
        tokens_first = lambda z: jnp.transpose(untile(z), (1, 2, 0))
        outs[4].append(tokens_first(kt32).reshape(db, t_new, n_heads, hd))
        outs[5].append(tokens_first(vt32).reshape(db, t_new, n_heads, hd))
        outs[6].append(tokens_first(kit32))
        outs[7].append(jnp.concatenate([hist, u.reshape(db, t_new, pw)], axis=1)[:, t_new:])
    return (y_p, y_s) + tuple(jnp.stack(o_) for o_ in outs)
```

```python
import functools
import math

import jax
import jax.numpy as jnp
import numpy as np
from jax import lax
from jax.experimental import pallas as pl
from jax.experimental.pallas import tpu as pltpu

F32, BF16, I32 = jnp.float32, jnp.bfloat16, jnp.int32

POOL_WINDOWS = (2, 4, 8, 16)
IDX_HEADS = 8
TOPK_MAX = 256
PEER_HEADS = 8
PEER_TOPK = 16
LN_EPS = 1e-5
N_ADA = 6

VMEM_LIMIT_BYTES = 56 * 1024 * 1024
SUBLANES = 8
LANES = 128
BF16_ROWS = 16
TOKEN_TILE = 256
PEER_TOKEN_TILE = 512
PEER_EXPERT_TILE = 1024
PEER_TOKEN_SPLIT = 2
SAMPLE_SEARCH_BITS = 4
HIST_ROWS = 16

INT_MIN = -(2 ** 31)
KEY_NEG_INF = -2139095041
NEG_BIG = -1e30
LOG2E = 1.4426950408889634


def _cparams(sem):
    return pltpu.CompilerParams(dimension_semantics=sem, vmem_limit_bytes=VMEM_LIMIT_BYTES)


def _dot(a, b):
    return jnp.dot(a, b, preferred_element_type=F32)


def _dot_nt(a, b):
    return lax.dot_general(a, b, (((1,), (1,)), ((), ())), preferred_element_type=F32)


def _key_to_float(key):
    return lax.bitcast_convert_type(key ^ ((key >> 31) & 0x7FFFFFFF), F32)


def _threshold_step(i, key, count_ge, n_sel, bits=1):
    unit = lax.shift_left(jnp.int32(1), 32 - bits * (i + 1))
    digit = jnp.zeros_like(key)
    for j in range(1, 2 ** bits):
        trial = key + j * unit
        ok = jnp.logical_or(count_ge(_key_to_float(trial)) >= n_sel, trial < KEY_NEG_INF)
        digit = digit + jnp.where(ok, 1, 0)
    return key + digit * unit


def _ada_kernel(c_ref, w_ref, b_ref, o_ref):
    o_ref[...] = _dot(c_ref[...].astype(BF16), w_ref[...].astype(BF16)) + b_ref[...]


def _ada(c_all, w_ada, b_ada):
    n, d = c_all.shape
    return pl.pallas_call(
        _ada_kernel,
        grid=(N_ADA,),
        in_specs=[pl.BlockSpec((n, d), lambda j: (0, 0)),
                  pl.BlockSpec((d, d), lambda j: (0, j)),
                  pl.BlockSpec((1, d), lambda j: (0, j))],
        out_specs=pl.BlockSpec((None, n, d), lambda j: (j, 0, 0)),
        out_shape=jax.ShapeDtypeStruct((N_ADA, n, d), F32),
        compiler_params=_cparams(("arbitrary",)),
        name="ada",
    )(c_all, w_ada, b_ada.reshape(1, -1))


def _grouped(x, mod, rows):
    nseq, l, d = x.shape
    if l >= rows:
        assert l % rows == 0
        per = l // rows
        x3 = x.reshape(nseq * per, rows, d)
        mod4 = jnp.broadcast_to(mod[:, :, None, None, :], (N_ADA, nseq, per, 1, d)).reshape(N_ADA, nseq * per, 1, d)
        return x3, mod4, 1
    assert rows % l == 0 and nseq % (rows // l) == 0
    return x, mod[:, :, None, :], rows // l


def _proj_kernel(x_ref, mod_ref, wn_ref, wki_ref, wg_ref, wt_ref,
                 u_ref, kb_ref, kib_ref, kt32_ref, vt32_ref, kit32_ref, vt_ref, qt_ref, qit_ref, wit_ref, ga_ref, gb_ref,
                 *, pw, aw, iw, idim, q_scale):
    ks, r, d = x_ref.shape
    h = (x_ref[...] * (1.0 + mod_ref[1]) + mod_ref[0]).reshape(ks * r, d).astype(BF16)
    u_ref[...] = _dot(h, wn_ref[:, 0:pw])
    kb_ref[...] = _dot(h, wn_ref[:, pw:pw + aw]).astype(BF16)
    kib_ref[...] = _dot(h, wki_ref[...]).astype(BF16)
    ga_ref[...] = _dot(h, wg_ref[:, 0:d])
    gb_ref[...] = _dot(h, wg_ref[:, d:2 * d])
    rows = np.cumsum([0, aw, aw, iw, 2 * SUBLANES, aw, idim])
    part = lambda j: _dot_nt(wt_ref[rows[j]:rows[j + 1], :], h)
    vt = part(0)
    vt32_ref[...] = vt
    vt_ref[...] = vt.astype(BF16)
    qt_ref[...] = (part(1) * q_scale).astype(BF16)
    qit_ref[...] = part(2).astype(BF16)
    wit_ref[...] = part(3)
    kt32_ref[...] = part(4)
    kit32_ref[...] = part(5)


def _proj(x3, mod4, ks, wn, wki, wg, wt, *, pw, aw, iw, idim, q_scale, seq_chunks=None):
    g, r, d = x3.shape
    tm = ks * r
    nb = g // ks
    n = g * r
    row = lambda cols: pl.BlockSpec((tm, cols), lambda i: (i, 0))
    col = lambda rows: pl.BlockSpec((None, rows, tm), lambda i: (i, 0, 0))
    full = lambda a: pl.BlockSpec(a.shape, lambda i: (0,) * a.ndim)
    if seq_chunks is None:
        state = lambda rows: ((nb, rows, tm), F32, col(rows))
    else:
        state = lambda rows: ((nb // seq_chunks, rows, seq_chunks * tm), F32,
                              pl.BlockSpec((None, rows, tm), lambda i: (i // seq_chunks, 0, i % seq_chunks)))
    outs = [((n, pw), F32, row(pw)), ((n, aw), BF16, row(aw)), ((n, idim), BF16, row(idim)),
            state(aw), state(aw), state(idim),
            ((nb, aw, tm), BF16, col(aw)), ((nb, aw, tm), BF16, col(aw)), ((nb, iw, tm), BF16, col(iw)),
            ((nb, 2 * SUBLANES, tm), F32, col(2 * SUBLANES)),
            ((n, d), F32, row(d)), ((n, d), F32, row(d))]
    return pl.pallas_call(
        functools.partial(_proj_kernel, pw=pw, aw=aw, iw=iw, idim=idim, q_scale=q_scale),
        grid=(nb,),
        in_specs=[pl.BlockSpec((ks, r, d), lambda i: (i, 0, 0)),
                  pl.BlockSpec((N_ADA, ks, 1, d), lambda i: (0, i, 0, 0)),
                  full(wn), full(wki), full(wg), full(wt)],
        out_specs=[o[2] for o in outs],
        out_shape=[jax.ShapeDtypeStruct(o[0], o[1]) for o in outs],
        compiler_params=_cparams(("arbitrary",)),
        name="in_proj",
    )(x3, mod4, wn, wki, wg, wt)


def _pool_windows(ext_ref, pos, pw_ref, ps_ref, a_ref):
    ks, rows, width = ext_ref.shape
    r = rows - HIST_ROWS
    gw = width // len(POOL_WINDOWS)
    for g, w in enumerate(POOL_WINDOWS):
        lo = g * gw
        cur = ext_ref[:, HIST_ROWS:HIST_ROWS + r, lo:lo + gw]
        acc = cur
        for j in range(1, w):
            acc = acc + ext_ref[:, HIST_ROWS - j:HIST_ROWS - j + r, lo:lo + gw]
        cnt = jnp.minimum(pos + 1, w).astype(F32)
        pooled = (acc / cnt - cur).reshape(ks * r, gw).astype(BF16)
        mixed = _dot(pooled, pw_ref[g])
        a_ref[:, lo:lo + gw] = (mixed * ps_ref[:, lo:lo + gw]).astype(BF16)


def _pool_prompt_kernel(u_ref, pw_ref, ps_ref, a_ref, ext_ref):
    i = pl.program_id(1)
    tp, width = u_ref.shape
    gw = width // len(POOL_WINDOWS)

    @pl.when(i == 0)
    def _():
        ext_ref[:, 0:HIST_ROWS, :] = jnp.zeros((1, HIST_ROWS, width), F32)

    @pl.when(i > 0)
    def _():
        ext_ref[:, 0:HIST_ROWS, :] = ext_ref[:, tp:tp + HIST_ROWS, :]

    ext_ref[:, HIST_ROWS:, :] = u_ref[...][None]
    pos = i * tp + lax.broadcasted_iota(I32, (1, tp, gw), 1)
    _pool_windows(ext_ref, pos, pw_ref, ps_ref, a_ref)


def _pool_prompt(u, pool_w, pool_scale):
    b, s, width = u.shape
    tp = TOKEN_TILE
    return pl.pallas_call(
        _pool_prompt_kernel,
        grid=(b, s // tp),
        in_specs=[pl.BlockSpec((None, tp, width), lambda bi, i: (bi, i, 0)),
                  pl.BlockSpec(pool_w.shape, lambda bi, i: (0, 0, 0)),
                  pl.BlockSpec(pool_scale.shape, lambda bi, i: (0, 0))],
        out_specs=pl.BlockSpec((None, tp, width), lambda bi, i: (bi, i, 0)),
        out_shape=jax.ShapeDtypeStruct((b, s, width), BF16),
        scratch_shapes=[pltpu.VMEM((1, HIST_ROWS + tp, width), F32)],
        compiler_params=_cparams(("arbitrary", "arbitrary")),
        name="pool_prompt",
    )(u, pool_w, pool_scale)


def _pool_sample_kernel(u_ref, hist_ref, pw_ref, ps_ref, a_ref, ext_ref, *, pos0):
    ks, t, width = u_ref.shape
    gw = width // len(POOL_WINDOWS)
    ext_ref[:, 0:HIST_ROWS, :] = hist_ref[...]
    ext_ref[:, HIST_ROWS:, :] = u_ref[...]
    pos = pos0 + lax.broadcasted_iota(I32, (1, t, gw), 1)
    _pool_windows(ext_ref, pos, pw_ref, ps_ref, a_ref)


def _pool_sample(u, hist16, pool_w, pool_scale, pos0):
    db, t, width = u.shape
    ks = TOKEN_TILE // t
    return pl.pallas_call(
        functools.partial(_pool_sample_kernel, pos0=pos0),
        grid=(db // ks,),
        in_specs=[pl.BlockSpec((ks, t, width), lambda i: (i, 0, 0)),
                  pl.BlockSpec((ks, HIST_ROWS, width), lambda i: (i, 0, 0)),
                  pl.BlockSpec(pool_w.shape, lambda i: (0, 0, 0)),
                  pl.BlockSpec(pool_scale.shape, lambda i: (0, 0))],
        out_specs=pl.BlockSpec((ks * t, width), lambda i: (i, 0)),
        out_shape=jax.ShapeDtypeStruct((db * t, width), BF16),
        scratch_shapes=[pltpu.VMEM((ks, HIST_ROWS + t, width), F32)],
        compiler_params=_cparams(("arbitrary",)),
        name="pool_sample",
    )(u, hist16, pool_w, pool_scale)


def _alibi_slopes(n_heads):
    return [2.0 ** (-8.0 * (h + 1) / n_heads) for h in range(n_heads)]


def _attn_prompt_kernel(qt_ref, qit_ref, wit_ref, kb_ref, vt_ref, kib_ref, o_ref,
                        sc_sc, scb_sc, qz_sc, acc_sc, lg_sc, al_sc, *, n_sel, idx_scale, n_heads, hd, idim, seq_len):
    qb = pl.program_id(1)
    tq = qt_ref.shape[-1]
    sk = tq
    nchunk = qb + 1
    t_idx = qb * tq + lax.broadcasted_iota(I32, (sk, tq), 1)
    s_loc = lax.broadcasted_iota(I32, (sk, tq), 0)

    def score_body(c, carry):
        kic = kib_ref[c]
        acc = jnp.zeros((sk, tq), F32)
        for h in range(IDX_HEADS):
            rel = _dot(kic, qit_ref[h * idim:(h + 1) * idim, :])
            acc = acc + wit_ref[h:h + 1, :] * jnp.maximum(rel, 0.0)
        sc = jnp.where(c * sk + s_loc <= t_idx, acc * idx_scale, -jnp.inf)
        sc_sc[c] = sc
        scb_sc[c] = sc.astype(BF16).reshape(sk // BF16_ROWS, BF16_ROWS, tq)
        return carry

    lax.fori_loop(0, nchunk, score_body, 0)

    def colsum(mask):
        return jnp.where(mask, 1.0, 0.0).reshape(sk // SUBLANES, SUBLANES, tq).sum(axis=0)

    @pl.when(nchunk % 2 == 1)
    def _():
        sc_sc[nchunk] = jnp.full((sk, tq), -jnp.inf, F32)
        scb_sc[nchunk] = jnp.full(scb_sc.shape[1:], -jnp.inf, BF16)

    def count(pred):
        def pair(c2, a):
            c = 2 * c2
            return a + colsum(pred(c, sc_sc[c])) + colsum(pred(c + 1, sc_sc[c + 1]))
        acc = lax.fori_loop(0, (nchunk + 1) // 2, pair, jnp.zeros((SUBLANES, tq), F32))
        return acc.sum(axis=0, keepdims=True)

    count_ge = lambda t: count(lambda c, s: s >= t)

    def count_ge_rounded(t):
        t_b = jnp.broadcast_to(t, (BF16_ROWS, tq)).astype(BF16)

        def pair(c2, a):
            for j in range(2):
                hit = jnp.where(scb_sc[2 * c2 + j] >= t_b, jnp.ones((), BF16), jnp.zeros((), BF16))
                rows = [hit[g] for g in range(sk // BF16_ROWS)]
                while len(rows) > 1:
                    rows = [x + y for x, y in zip(rows[0::2], rows[1::2])]
                a = a + rows[0].astype(F32)
            return a
        acc = lax.fori_loop(0, (nchunk + 1) // 2, pair, jnp.zeros((BF16_ROWS, tq), F32))
        return acc.sum(axis=0, keepdims=True)

    def wide_key(k16):
        return lax.shift_left(k16, 16) | jnp.where(k16 < 0, 0xFFFF, 0)

    def coarse(i, k16):
        trial = k16 + lax.shift_left(jnp.int32(1), 15 - i)
        k32 = wide_key(trial)
        ok = jnp.logical_or(count_ge_rounded(_key_to_float(k32)) >= n_sel, k32 < KEY_NEG_INF)
        return jnp.where(ok, trial, k16)

    k32 = wide_key(lax.fori_loop(0, 16, coarse, jnp.full((1, tq), -(2 ** 15), I32)))
    half_step, step = 2 ** 15, 2 ** 16

    def fine(i, bracket):
        lo, hi = bracket
        mid = lo + ((hi - lo) >> 1)
        ok = count_ge(_key_to_float(mid)) >= n_sel
        open_ = hi - lo > 1
        return (jnp.where(jnp.logical_and(open_, ok), mid, lo),
                jnp.where(jnp.logical_and(open_, jnp.logical_not(ok)), mid, hi))

    n_fine = int(math.ceil(math.log2(half_step + step)))
    tau_key, _ = lax.fori_loop(0, n_fine, fine, (jnp.maximum(k32 - half_step, KEY_NEG_INF), k32 + step))
    tau = _key_to_float(tau_key)

    tie_rows = jnp.logical_and(count_ge(tau) > n_sel, tau_key > KEY_NEG_INF)
    has_tie = jnp.max(jnp.where(tie_rows, 1.0, 0.0)) > 0.5

    @pl.when(has_tie)
    def _():
        need = n_sel - count(lambda c, s: s > tau)
        nbits = max(1, int(math.ceil(math.log2(seq_len))))

        def first_enough(i, j):
            step = lax.shift_left(jnp.int32(1), nbits - 1 - i)
            upto = count(lambda c, s: jnp.logical_and(s == tau, c * sk + s_loc <= j + step - 1))
            return jnp.where(upto < need, j + step, j)

        last = lax.fori_loop(0, nbits, first_enough, jnp.zeros((1, tq), I32))

        def demote(c, carry):
            s = sc_sc[c]
            drop = jnp.logical_and(jnp.logical_and(tie_rows, s == tau), c * sk + s_loc > last)
            sc_sc[c] = jnp.where(drop, -jnp.inf, s)
            return carry

        lax.fori_loop(0, nchunk, demote, 0)

    pair_rows = lax.broadcasted_iota(I32, (2 * hd, tq), 0)
    for h in range(n_heads):
        blk = qt_ref[(h // 2) * 2 * hd:(h // 2 + 1) * 2 * hd, :]
        keep = pair_rows < hd if h % 2 == 0 else pair_rows >= hd
        qz_sc[h] = jnp.where(keep, blk, jnp.zeros_like(blk))
    acc_sc[...] = jnp.zeros(acc_sc.shape, F32)
    slopes = [s * LOG2E for s in _alibi_slopes(n_heads)]
    for h in range(n_heads):
        al_sc[h] = slopes[h] * s_loc.astype(F32)

    def att_chunk(c, j, m_all, l_all):
        sel = jnp.logical_and(sc_sc[c] >= tau, c * sk + s_loc <= t_idx)
        off = ((c - qb) * sk).astype(F32)
        m_out, l_out = [], []
        for h in range(n_heads):
            lg = jnp.where(sel, lg_sc[j, h] + al_sc[h], -jnp.inf)
            m_new = jnp.maximum(m_all[h], jnp.max(lg, axis=0, keepdims=True) + slopes[h] * off)
            alpha = jnp.exp2(m_all[h] - m_new)
            p = jnp.exp2(lg - (m_new - slopes[h] * off))
            l_out.append(alpha * l_all[h] + jnp.sum(p, axis=0, keepdims=True))
            m_out.append(m_new)
            pv = _dot(vt_ref[c, h * hd:(h + 1) * hd, :], p.astype(BF16))
            acc_sc[h * hd:(h + 1) * hd, :] = alpha * acc_sc[h * hd:(h + 1) * hd, :] + pv
        return tuple(m_out), tuple(l_out)

    def att_pair(c2, carry):
        for j in range(2):
            for h in range(n_heads):
                lg_sc[j, h] = _dot(kb_ref[2 * c2 + j, :, (h // 2) * 2 * hd:(h // 2 + 1) * 2 * hd], qz_sc[h])
        for j in range(2):
            carry = att_chunk(2 * c2 + j, j, *carry)
        return carry

    init = (tuple(jnp.full((1, tq), NEG_BIG, F32) for _ in range(n_heads)),
            tuple(jnp.zeros((1, tq), F32) for _ in range(n_heads)))
    _, l_all = lax.fori_loop(0, (nchunk + 1) // 2, att_pair, init)
    out_t = jnp.concatenate([acc_sc[h * hd:(h + 1) * hd, :] / l_all[h] for h in range(n_heads)], axis=0)
    o_ref[...] = out_t.T.astype(BF16)


def _attn_prompt(qt, qit, wit, kb, vt, kib, *, n_sel, idx_scale, n_heads, hd, idim):
    b, nck, aw, tq = qt.shape
    assert nck % 2 == 0
    iw = qit.shape[2]
    resident = lambda shape: pl.BlockSpec((None,) + shape, lambda bi, i: (bi, 0, 0, 0), pipeline_mode=pl.Buffered(1))
    tile = lambda rows: pl.BlockSpec((None, None, rows, tq), lambda bi, i: (bi, i, 0, 0))
    return pl.pallas_call(
        functools.partial(_attn_prompt_kernel, n_sel=n_sel, idx_scale=idx_scale, n_heads=n_heads, hd=hd, idim=idim,
                          seq_len=nck * tq),
        grid=(b, nck),
        in_specs=[tile(aw), tile(iw), tile(2 * SUBLANES),
                  resident((nck, tq, aw)), resident((nck, aw, tq)), resident((nck, tq, idim))],
        out_specs=pl.BlockSpec((None, tq, aw), lambda bi, i: (bi, i, 0)),
        out_shape=jax.ShapeDtypeStruct((b, nck * tq, aw), BF16),
        scratch_shapes=[pltpu.VMEM((nck, tq, tq), F32),
                        pltpu.VMEM((nck, tq // BF16_ROWS, BF16_ROWS, tq), BF16),
                        pltpu.VMEM((n_heads, 2 * hd, tq), BF16),
                        pltpu.VMEM((aw, tq), F32),
                        pltpu.VMEM((2, n_heads, tq, tq), F32),
                        pltpu.VMEM((n_heads, tq, tq), F32)],
        compiler_params=_cparams(("arbitrary", "arbitrary")),
        name="attn_prompt",
    )(qt, qit, wit, kb, vt, kib)


def _attn_sample_kernel(pt_ref, qbd_ref, qi_ref, wi_ref, knew_ref, vnew_ref, kinew_ref, *rest,
                        n_pages, page, n_sel, idx_scale, n_heads, hd, t_new):
    del pt_ref
    ki_refs, k_refs, v_refs = rest[0:n_pages], rest[n_pages:2 * n_pages], rest[2 * n_pages:3 * n_pages]
    o_ref, sc_sc, lg_sc = rest[3 * n_pages:]
    past = n_pages * page
    width = past + page
    group = n_heads // 2
    row = lax.broadcasted_iota(I32, (t_new, page), 0)
    lane = lax.broadcasted_iota(I32, (t_new, page), 1)
    wi = wi_ref[...]

    def scores(kit):
        rel = jnp.maximum(_dot(qi_ref[...], kit), 0.0)
        return jnp.sum(rel.reshape(IDX_HEADS, t_new, rel.shape[1]) * wi, axis=0) * idx_scale

    def page_pair(refs, pp, head_lo=None):
        if head_lo is None:
            parts = [refs[2 * pp + j][...] for j in range(2)]
        else:
            parts = [refs[2 * pp + j][head_lo:head_lo + group].reshape(group * hd, page) for j in range(2)]
        return jnp.concatenate(parts, axis=1).astype(BF16)

    for pp in range(n_pages // 2):
        sc_sc[:, 2 * pp * page:(2 * pp + 2) * page] = scores(page_pair(ki_refs, pp))
    sc_sc[:, past:width] = jnp.where(lane <= row, scores(kinew_ref[...]), -jnp.inf)

    def count(pred):
        return jnp.sum(jnp.where(pred(sc_sc[...]), 1.0, 0.0), axis=1, keepdims=True)

    count_ge = lambda t: count(lambda s: s >= t)
    tau_key = lax.fori_loop(0, 32 // SAMPLE_SEARCH_BITS,
                            lambda i, k: _threshold_step(i, k, count_ge, n_sel, SAMPLE_SEARCH_BITS),
                            jnp.full((t_new, 1), INT_MIN, I32))
    tau = _key_to_float(tau_key)

    tie_rows = jnp.logical_and(count_ge(tau) > n_sel, tau_key > KEY_NEG_INF)
    has_tie = jnp.max(jnp.where(tie_rows, 1.0, 0.0)) > 0.5
    pos = lax.broadcasted_iota(I32, (t_new, width), 1)

    @pl.when(has_tie)
    def _():
        need = n_sel - count(lambda s: s > tau)
        nbits = max(1, int(math.ceil(math.log2(width))))

        def first_enough(i, j):
            step = lax.shift_left(jnp.int32(1), nbits - 1 - i)
            upto = count(lambda s: jnp.logical_and(s == tau, pos <= j + step - 1))
            return jnp.where(upto < need, j + step, j)

        last = lax.fori_loop(0, nbits, first_enough, jnp.zeros((t_new, 1), I32))
        s = sc_sc[...]
        drop = jnp.logical_and(jnp.logical_and(tie_rows, s == tau), pos > last)
        sc_sc[...] = jnp.where(drop, -jnp.inf, s)

    slopes = [s * LOG2E for s in _alibi_slopes(n_heads)]
    t_pos = past + lax.broadcasted_iota(I32, (t_new, width), 0)
    sel = jnp.logical_and(sc_sc[...] >= tau, pos <= t_pos)
    bias = jnp.where(sel, 0.0, -jnp.inf)
    dist = (t_pos - pos).astype(F32)
    spans = [(2 * pp * page, 2 * page) for pp in range(n_pages // 2)] + [(past, page)]
    for g in range(2):
        for pp, (lo, n) in enumerate(spans):
            kt = page_pair(k_refs, pp, g * group) if pp < n_pages // 2 else knew_ref[g]
            lg = _dot(qbd_ref[g], kt).reshape(group, t_new, n)
            for hl in range(group):
                h = g * group + hl
                lg_sc[h, :, lo:lo + n] = lg[hl] - slopes[h] * dist[:, lo:lo + n] + bias[:, lo:lo + n]
    for g in range(2):
        probs, denoms = [], []
        for hl in range(group):
            lg = lg_sc[g * group + hl]
            pr = jnp.exp2(lg - jnp.max(lg, axis=1, keepdims=True))
            denoms.append(jnp.sum(pr, axis=1, keepdims=True))
            probs.append(pr)
        pg = jnp.concatenate(probs, axis=0).astype(BF16)
        acc = jnp.zeros((group * t_new, group * hd), F32)
        for pp, (lo, n) in enumerate(spans):
            vt = page_pair(v_refs, pp, g * group) if pp < n_pages // 2 else vnew_ref[g]
            acc = acc + _dot_nt(pg[:, lo:lo + n], vt)
        for hl in range(group):
            o_ref[g * group + hl] = acc[hl * t_new:(hl + 1) * t_new, hl * hd:(hl + 1) * hd] / denoms[hl]


def _attn_sample(page_table, qbd, qi2, wi3, knew, vnew, kinew, k_t, v_t, ki_t, layer, *, n_sel, idx_scale):
    db = qbd.shape[0]
    n_pages = page_table.shape[1]
    n_heads, hd, page = k_t.shape[2:]
    idim = ki_t.shape[2]
    t_new = wi3.shape[2]
    assert n_pages % 2 == 0 and n_heads % 2 == 0
    per_seq = lambda a: pl.BlockSpec((None,) + a.shape[1:], lambda b, pt: (b,) + (0,) * (a.ndim - 1))
    kv_page = lambda j: pl.BlockSpec((None, None, n_heads, hd, page), lambda b, pt: (layer, pt[b, j], 0, 0, 0))
    ki_page = lambda j: pl.BlockSpec((None, None, idim, page), lambda b, pt: (layer, pt[b, j], 0, 0))
    in_specs = ([per_seq(qbd), per_seq(qi2), per_seq(wi3), per_seq(knew), per_seq(vnew), per_seq(kinew)]
                + [ki_page(j) for j in range(n_pages)]
                + [kv_page(j) for j in range(n_pages)]
                + [kv_page(j) for j in range(n_pages)])
    grid_spec = pltpu.PrefetchScalarGridSpec(
        num_scalar_prefetch=1,
        grid=(db,),
        in_specs=in_specs,
        out_specs=pl.BlockSpec((None, n_heads, t_new, hd), lambda b, pt: (b, 0, 0, 0)),
        scratch_shapes=[pltpu.VMEM((t_new, (n_pages + 1) * page), F32),
                        pltpu.VMEM((n_heads, t_new, (n_pages + 1) * page), F32)],
    )
    return pl.pallas_call(
        functools.partial(_attn_sample_kernel, n_pages=n_pages, page=page, n_sel=n_sel, idx_scale=idx_scale,
                          n_heads=n_heads, hd=hd, t_new=t_new),
        grid_spec=grid_spec,
        out_shape=jax.ShapeDtypeStruct((db, n_heads, t_new, hd), F32),
        compiler_params=_cparams(("arbitrary",)),
        name="attn_sample",
    )(page_table, qbd, qi2, wi3, knew, vnew, kinew, *([ki_t] * n_pages), *([k_t] * n_pages), *([v_t] * n_pages))


def _layer_norm(y, g, b):
    mu = jnp.mean(y, axis=-1, keepdims=True)
    var = jnp.mean(jnp.square(y - mu), axis=-1, keepdims=True)
    return (y - mu) * lax.rsqrt(var + LN_EPS) * g + b


def _merge_kernel(x_ref, mod_ref, a_ref, o_ref, ga_ref, gb_ref, wa_ref, wb_ref, wo_ref, g_ref, b_ref, x1_ref, *, alpha):
    ks, r, d = x_ref.shape
    merged = (jax.nn.sigmoid(ga_ref[...]) * _dot(a_ref[...], wa_ref[...])
              + jax.nn.sigmoid(gb_ref[...]) * _dot(o_ref[...], wb_ref[...]))
    upd = _dot(merged.astype(BF16), wo_ref[...]).reshape(ks, r, d)
    y = alpha * x_ref[...] + mod_ref[2] * upd
    x1_ref[...] = _layer_norm(y, g_ref[...], b_ref[...])


def _merge(x3, mod4, ks, a, o, ga, gb, wa, wb, wo, ln_g, ln_b, alpha):
    g, r, d = x3.shape
    tm = ks * r
    row = lambda arr: pl.BlockSpec((tm, arr.shape[1]), lambda i: (i, 0))
    full = lambda arr: pl.BlockSpec(arr.shape, lambda i: (0,) * arr.ndim)
    return pl.pallas_call(
        functools.partial(_merge_kernel, alpha=alpha),
        grid=(g // ks,),
        in_specs=[pl.BlockSpec((ks, r, d), lambda i: (i, 0, 0)),
                  pl.BlockSpec((N_ADA, ks, 1, d), lambda i: (0, i, 0, 0)),
                  row(a), row(o), row(ga), row(gb), full(wa), full(wb), full(wo), full(ln_g), full(ln_b)],
        out_specs=pl.BlockSpec((ks, r, d), lambda i: (i, 0, 0)),
        out_shape=jax.ShapeDtypeStruct((g, r, d), F32),
        compiler_params=_cparams(("arbitrary",)),
        name="merge_ln1",
    )(x3, mod4, a, o, ga, gb, wa, wb, wo, ln_g, ln_b)


def _top_rows_exact(s, n_top):
    rows = s.shape[0]
    row_id = lax.broadcasted_iota(I32, s.shape, 0)
    rank = jnp.full(s.shape, n_top, I32)
    vals = []
    for r in range(n_top):
        m = jnp.max(s, axis=0, keepdims=True)
        idx = jnp.min(jnp.where(s == m, row_id, rows), axis=0, keepdims=True)
        hit = row_id == idx
        vals.append(m)
        rank = jnp.where(hit, r, rank)
        s = jnp.where(hit, -jnp.inf, s)
    return jnp.concatenate(vals, axis=0), rank


def _top_rows_distinct(s, n_top):
    rank = jnp.full(s.shape, n_top, I32)
    vals = []
    for r in range(n_top):
        m = jnp.max(s, axis=0, keepdims=True)
        hit = s == m
        vals.append(m)
        rank = jnp.where(hit, r, rank)
        s = jnp.where(hit, -jnp.inf, s)
    return jnp.concatenate(vals, axis=0), rank


def _rank_ties(rank, n_top):
    taken = jnp.sum(jnp.where(rank < n_top, 1.0, 0.0), axis=0, keepdims=True)
    return jnp.max(jnp.where(taken != n_top, 1.0, 0.0)) > 0.5


PEER_PAIRS = [(a, b) for a in range(PEER_TOPK) for b in range(PEER_TOPK) if (a + 1) * (b + 1) <= PEER_TOPK]
PEER_CAND_ROWS = -(-len(PEER_PAIRS) // SUBLANES) * SUBLANES


def _peer_kernel(x_ref, mod_ref, wq_ref, kt_ref, u_ref, vt_ref, g_ref, b_ref, y_ref,
                 h2_sc, st_sc, e0_sc, nb_sc, e1_sc, r1_sc, acc_sc, sv_sc, rk_sc, ts_sc, r2_sc, act_sc, w_sc,
                 *, alpha, n_keys):
    ks, r, d = x_ref.shape
    tm = ks * r
    step = pl.program_id(1)
    sub = PEER_EXPERT_TILE // n_keys

    @pl.when(step == 0)
    def _():
        h2 = (x_ref[...] * (1.0 + mod_ref[4]) + mod_ref[3]).reshape(tm, d).astype(BF16)
        h2_sc[...] = h2
        qp = _dot(h2, wq_ref[...]).astype(BF16)
        st_sc[...] = _dot_nt(kt_ref[...], qp)
        acc_sc[...] = jnp.zeros(acc_sc.shape, F32)

        def route(h, carry):
            s0 = st_sc[pl.ds(pl.multiple_of(h * 2 * n_keys, 2 * n_keys), n_keys), :]
            s1 = st_sc[pl.ds(pl.multiple_of(h * 2 * n_keys + n_keys, n_keys), n_keys), :]

            def halves(top):
                sv_sc[0], rk_sc[0] = top(s0, PEER_TOPK)
                sv_sc[1], rk_sc[1] = top(s1, PEER_TOPK)

            halves(_top_rows_distinct)

            @pl.when(jnp.logical_or(_rank_ties(rk_sc[0], PEER_TOPK), _rank_ties(rk_sc[1], PEER_TOPK)))
            def _():
                halves(_top_rows_exact)

            v0, v1, rank0 = sv_sc[0], sv_sc[1], rk_sc[0]
            pad = [jnp.full((PEER_CAND_ROWS - len(PEER_PAIRS), tm), -jnp.inf, F32)]
            cand = jnp.concatenate([v0[a:a + 1] + v1[b:b + 1] for a, b in PEER_PAIRS] + pad, axis=0)

            def pairs(top):
                ts_sc[...], r2_sc[...] = top(cand, PEER_TOPK)

            pairs(_top_rows_distinct)

            @pl.when(_rank_ties(r2_sc[...], PEER_TOPK))
            def _():
                pairs(_top_rows_exact)

            top_s, rank2 = ts_sc[...], r2_sc[...]
            z = jnp.sum(jnp.exp(top_s - top_s[0:1]), axis=0, keepdims=True)
            took = jnp.where(rank2 < PEER_TOPK, 1, 0)
            take = jnp.zeros(rank0.shape, I32)
            for a in range(PEER_TOPK):
                n_a = sum(took[i:i + 1] for i, (pa, _) in enumerate(PEER_PAIRS) if pa == a)
                take = jnp.where(rank0 == a, n_a, take)
            e0_sc[h] = (jnp.exp(s0 - v0[0:1]) / z * 0.5).reshape(n_keys // SUBLANES, SUBLANES, tm)
            nb_sc[h] = take.astype(F32).reshape(n_keys // SUBLANES, SUBLANES, tm)
            e1_sc[h] = jnp.exp(s1 - v1[0:1]).astype(BF16).reshape(n_keys // BF16_ROWS, BF16_ROWS, tm)
            r1_sc[h] = rk_sc[1].astype(F32).astype(BF16).reshape(n_keys // BF16_ROWS, BF16_ROWS, tm)
            return carry

        lax.fori_loop(0, PEER_HEADS, route, 0)

    n_tok = tm // PEER_TOKEN_SPLIT
    for t in range(PEER_TOKEN_SPLIT):
        act_sc[t] = _dot_nt(u_ref[...], h2_sc[t * n_tok:(t + 1) * n_tok, :])
    for t in range(PEER_TOKEN_SPLIT):
        lanes = slice(t * n_tok, (t + 1) * n_tok)
        for ii in range(sub):
            act = act_sc[t, ii * n_keys:(ii + 1) * n_keys, :]
            gate = jnp.zeros((n_keys // BF16_ROWS, BF16_ROWS, n_tok), BF16)
            for h in range(PEER_HEADS):
                take_i = jnp.broadcast_to(nb_sc[h, step, ii:ii + 1, lanes], (BF16_ROWS, n_tok)).astype(BF16)
                e0_i = jnp.broadcast_to(e0_sc[h, step, ii:ii + 1, lanes], (BF16_ROWS, n_tok)).astype(BF16)
                e1 = e1_sc[h, :, :, lanes]
                gate = gate + jnp.where(r1_sc[h, :, :, lanes] < take_i, e1, jnp.zeros_like(e1)) * e0_i
            gelu2 = act * (1.0 + lax.erf(act * (2.0 ** -0.5)))
            w = gelu2.astype(BF16).reshape(n_keys // BF16_ROWS, BF16_ROWS, n_tok) * gate
            w_sc[t, ii * n_keys:(ii + 1) * n_keys, :] = w.reshape(n_keys, n_tok)
    for t in range(PEER_TOKEN_SPLIT):
        acc_sc[:, t * n_tok:(t + 1) * n_tok] += _dot(vt_ref[...], w_sc[t])

    @pl.when(step == pl.num_programs(1) - 1)
    def _():
        f = acc_sc[...].T.reshape(ks, r, d)
        y = alpha * x_ref[...] + mod_ref[5] * f
        y_ref[...] = _layer_norm(y, g_ref[...], b_ref[...])


def _peer(x3, mod4, ks, wq, kt, u_b, vt_b, ln_g, ln_b, alpha, n_keys):
    g, r, d = x3.shape
    tm = ks * r
    n_exp = u_b.shape[0]
    assert PEER_EXPERT_TILE == SUBLANES * n_keys and n_exp % PEER_EXPERT_TILE == 0
    full = lambda arr: pl.BlockSpec(arr.shape, lambda i, e: (0,) * arr.ndim)
    return pl.pallas_call(
        functools.partial(_peer_kernel, alpha=alpha, n_keys=n_keys),
        grid=(g // ks, n_exp // PEER_EXPERT_TILE),
        in_specs=[pl.BlockSpec((ks, r, d), lambda i, e: (i, 0, 0)),
                  pl.BlockSpec((N_ADA, ks, 1, d), lambda i, e: (0, i, 0, 0)),
                  full(wq), full(kt),
                  pl.BlockSpec((PEER_EXPERT_TILE, d), lambda i, e: (e, 0)),
                  pl.BlockSpec((d, PEER_EXPERT_TILE), lambda i, e: (0, e)),
                  full(ln_g), full(ln_b)],
        out_specs=pl.BlockSpec((ks, r, d), lambda i, e: (i, 0, 0)),
        out_shape=jax.ShapeDtypeStruct((g, r, d), F32),
        scratch_shapes=[pltpu.VMEM((tm, d), BF16),
                        pltpu.VMEM((kt.shape[0], tm), F32),
                        pltpu.VMEM((PEER_HEADS, n_keys // SUBLANES, SUBLANES, tm), F32),
                        pltpu.VMEM((PEER_HEADS, n_keys // SUBLANES, SUBLANES, tm), F32),
                        pltpu.VMEM((PEER_HEADS, n_keys // BF16_ROWS, BF16_ROWS, tm), BF16),
                        pltpu.VMEM((PEER_HEADS, n_keys // BF16_ROWS, BF16_ROWS, tm), BF16),
                        pltpu.VMEM((d, tm), F32),
                        pltpu.VMEM((2, PEER_TOPK, tm), F32),
                        pltpu.VMEM((2, n_keys, tm), I32),
                        pltpu.VMEM((PEER_TOPK, tm), F32),
                        pltpu.VMEM((PEER_CAND_ROWS, tm), I32),
                        pltpu.VMEM((PEER_TOKEN_SPLIT, PEER_EXPERT_TILE, tm // PEER_TOKEN_SPLIT), F32),
                        pltpu.VMEM((PEER_TOKEN_SPLIT, PEER_EXPERT_TILE, tm // PEER_TOKEN_SPLIT), BF16)],
        compiler_params=_cparams(("arbitrary", "arbitrary")),
        name="peer_ln2",
    )(x3, mod4, wq, kt, u_b, vt_b, ln_g, ln_b)


def _layer_weights(w_in, pool_w, w_branch_a, w_branch_b, w_out, peer_wq, peer_sub_keys, peer_u, peer_v,
                   *, pw, aw, iw, idim):
    d = w_in.shape[0]
    o = np.cumsum([0, pw, aw, aw, aw, iw, idim, IDX_HEADS, d, d])
    cut = lambda j: w_in[:, o[j]:o[j + 1]]
    u_w, q_w, k_w, v_w, qi_w, ki_w, wi_w, ga_w, gb_w = (cut(j) for j in range(9))
    wn = jnp.concatenate([u_w, k_w], axis=1).astype(BF16)
    wg = jnp.concatenate([ga_w, gb_w], axis=1).astype(BF16)
    wi_pad = jnp.pad(wi_w, ((0, 0), (0, 2 * SUBLANES - IDX_HEADS)))
    wt = jnp.concatenate([v_w, q_w, qi_w, wi_pad, k_w, ki_w], axis=1).T.astype(BF16)
    n_keys, half = peer_sub_keys.shape[1], peer_sub_keys.shape[2]
    kt = jnp.einsum("hg,cb,ckd->hckgbd", jnp.eye(PEER_HEADS, dtype=F32), jnp.eye(2, dtype=F32), peer_sub_keys)
    kt = kt.reshape(PEER_HEADS * 2 * n_keys, PEER_HEADS * 2 * half).astype(BF16)
    return dict(wn=wn, wki=ki_w.astype(BF16), wg=wg, wt=wt, pool_w=pool_w.astype(BF16),
                wa=w_branch_a.astype(BF16), wb=w_branch_b.astype(BF16), wo=w_out.astype(BF16),
                wq=peer_wq.astype(BF16), kt=kt, u_b=peer_u.astype(BF16), vt_b=peer_v.T.astype(BF16))


def kernel(x_prompt, x_sample, cache_k, cache_v, cache_kidx, state_pool, page_table, c_prompt, c_sample,
           w_ada, b_ada, w_in, pool_w, pool_scale, w_branch_a, w_branch_b, w_out, ln1_g, ln1_b,
           peer_wq, peer_sub_keys, peer_u, peer_v, ln2_g, ln2_b):
    depth = w_ada.shape[0]
    b, s, d = x_prompt.shape
    db, t_new, _ = x_sample.shape
    page, n_heads, hd = cache_k.shape[2:]
    idim = cache_kidx.shape[3]
    pool_hist, pw = state_pool.shape[2:]
    n_pages = page_table.shape[1]
    past = n_pages * page
    aw, iw = n_heads * hd, IDX_HEADS * idim
    n_keys = peer_sub_keys.shape[2]
    alpha = (2 * depth) ** 0.25
    idx_scale = float(iw) ** -0.5
    q_scale = float(hd) ** -0.5 * LOG2E
    assert t_new == SUBLANES and n_heads == SUBLANES and pool_hist < HIST_ROWS and s % TOKEN_TILE == 0

    n_c = b + db
    c_all = jnp.pad(jnp.concatenate([c_prompt, c_sample], axis=0), ((0, -n_c % SUBLANES), (0, 0)))
    y_p, y_s = x_prompt, x_sample
    k_t = jnp.transpose(cache_k, (0, 1, 3, 4, 2))
    v_t = jnp.transpose(cache_v, (0, 1, 3, 4, 2))
    ki_t = jnp.transpose(cache_kidx, (0, 1, 3, 2))
    outs = [[] for _ in range(8)]
    for l in range(depth):
        w = _layer_weights(w_in[l], pool_w[l], w_branch_a[l], w_branch_b[l], w_out[l], peer_wq[l], peer_sub_keys[l],
                           peer_u[l], peer_v[l], pw=pw, aw=aw, iw=iw, idim=idim)
        mod = _ada(c_all, w_ada[l], b_ada[l])
        mod_p, mod_s = mod[:, :b], mod[:, b:n_c]
        ps, g1, b1, g2, b2 = pool_scale[l][None], ln1_g[l][None], ln1_b[l][None], ln2_g[l][None], ln2_b[l][None]
        proj = functools.partial(_proj, wn=w["wn"], wki=w["wki"], wg=w["wg"], wt=w["wt"],
                                 pw=pw, aw=aw, iw=iw, idim=idim, q_scale=q_scale)

        x3, mod4, ks = _grouped(y_p, mod_p, TOKEN_TILE)
        nck = s // TOKEN_TILE
        u, kb, kib, kt32, vt32, kit32, vt, qt, qit, wit, ga, gb = proj(x3, mod4, ks, seq_chunks=nck)
        a = _pool_prompt(u.reshape(b, s, pw), w["pool_w"], ps)
        o = _attn_prompt(qt.reshape(b, nck, aw, TOKEN_TILE), qit.reshape(b, nck, iw, TOKEN_TILE),
                         wit.reshape(b, nck, 2 * SUBLANES, TOKEN_TILE), kb.reshape(b, nck, TOKEN_TILE, aw),
                         vt.reshape(b, nck, aw, TOKEN_TILE), kib.reshape(b, nck, TOKEN_TILE, idim),
                         n_sel=min(TOPK_MAX, s // 4), idx_scale=idx_scale, n_heads=n_heads, hd=hd, idim=idim)
        x1 = _merge(x3, mod4, ks, a.reshape(b * s, pw), o.reshape(b * s, aw), ga, gb,
                    w["wa"], w["wb"], w["wo"], g1, b1, alpha)
        x3p, mod4p, ksp = _grouped(x1.reshape(b, s, d), mod_p, PEER_TOKEN_TILE)
        y_p = _peer(x3p, mod4p, ksp, w["wq"], w["kt"], w["u_b"], w["vt_b"], g2, b2, alpha, n_keys).reshape(b, s, d)
        heads_last = lambda z: jnp.transpose(z.reshape(z.shape[0], n_heads, hd, z.shape[2]), (0, 3, 1, 2))
        outs[0].append(heads_last(kt32))
        outs[1].append(heads_last(vt32))
        outs[2].append(jnp.transpose(kit32, (0, 2, 1)))
        outs[3].append(u.reshape(b, s, pw)[:, s - pool_hist:])

        x3, mod4, ks = _grouped(y_s, mod_s, TOKEN_TILE)
        u, kb, kib, kt32, vt32, kit32, vt, qt, qit, wit, ga, gb = proj(x3, mod4, ks)
        hist = state_pool[l]
        hist16 = jnp.pad(hist, ((0, 0), (HIST_ROWS - pool_hist, 0), (0, 0)))
        a = _pool_sample(u.reshape(db, t_new, pw), hist16, w["pool_w"], ps, past)
        untile = lambda z: jnp.moveaxis(z, 0, 1).reshape(z.shape[1], db, t_new)
        group = n_heads // 2
        q_s = jnp.transpose(untile(qt).reshape(2, group, hd, db, t_new), (3, 0, 1, 4, 2))
        qbd = jnp.einsum("bgltd,lm->bgltmd", q_s, jnp.eye(group, dtype=BF16)).reshape(db, 2, group * t_new, group * hd)
        qi2 = jnp.transpose(untile(qit).reshape(IDX_HEADS, idim, db, t_new), (2, 0, 3, 1)).reshape(db, IDX_HEADS * t_new, idim)
        wi3 = jnp.transpose(untile(wit)[:IDX_HEADS], (1, 0, 2))[..., None]
        lane_pad = lambda z: jnp.pad(z, ((0, 0),) * (z.ndim - 1) + ((0, page - t_new),))
        knew = lane_pad(jnp.transpose(kb.reshape(db, t_new, 2, group * hd), (0, 2, 3, 1)))
        vnew = lane_pad(jnp.transpose(untile(vt).reshape(2, group * hd, db, t_new), (2, 0, 1, 3)))
        kinew = lane_pad(jnp.transpose(kib.reshape(db, t_new, idim), (0, 2, 1)))
        o = _attn_sample(page_table, qbd, qi2, wi3, knew, vnew, kinew, k_t, v_t, ki_t, l,
                         n_sel=min(TOPK_MAX, (past + t_new) // 4), idx_scale=idx_scale)
        o = jnp.transpose(o, (0, 2, 1, 3)).reshape(db * t_new, aw).astype(BF16)
        x1 = _merge(x3, mod4, ks, a, o, ga, gb, w["wa"], w["wb"], w["wo"], g1, b1, alpha)
        x3p, mod4p, ksp = _grouped(x1, mod_s, PEER_TOKEN_TILE)
        y_s = _peer(x3p, mod4p, ksp, w["wq"], w["kt"], w["u_b"], w["vt_b"], g2, b2, alpha, n_keys)
        tokens_first = lambda z: jnp.transpose(untile(z), (1, 2, 0))
        outs[4].append(tokens_first(kt32).reshape(db, t_new, n_heads, hd))
        outs[5].append(tokens_first(vt32).reshape(db, t_new, n_heads, hd))
        outs[6].append(tokens_first(kit32))
        outs[7].append(jnp.concatenate([hist, u.reshape(db, t_new, pw)], axis=1)[:, t_new:])
    return (y_p, y_s) + tuple(jnp.stack(o_) for o_ in outs)
```

```python
import functools
import math

import jax
import jax.numpy as jnp
import numpy as np
from jax import lax
from jax.experimental import pallas as pl
from jax.experimental.pallas import tpu as pltpu

F32, BF16, I32 = jnp.float32, jnp.bfloat16, jnp.int32

POOL_WINDOWS = (2, 4, 8, 16)
IDX_HEADS = 8
TOPK_MAX = 256
PEER_HEADS = 8
PEER_TOPK = 16
LN_EPS = 1e-5
N_ADA = 6

VMEM_LIMIT_BYTES = 56 * 1024 * 1024
SUBLANES = 8
LANES = 128
BF16_ROWS = 16
TOKEN_TILE = 256
PEER_TOKEN_TILE = 512
PEER_EXPERT_TILE = 1024
PEER_TOKEN_SPLIT = 2
SAMPLE_SEARCH_BITS = 4
HIST_ROWS = 16

INT_MIN = -(2 ** 31)
KEY_NEG_INF = -2139095041
NEG_BIG = -1e30
LOG2E = 1.4426950408889634


def _cparams(sem):
    return pltpu.CompilerParams(dimension_semantics=sem, vmem_limit_bytes=VMEM_LIMIT_BYTES)


def _dot(a, b):
    return jnp.dot(a, b, preferred_element_type=F32)


def _dot_nt(a, b):
    return lax.dot_general(a, b, (((1,), (1,)), ((), ())), preferred_element_type=F32)


def _key_to_float(key):
    return lax.bitcast_convert_type(key ^ ((key >> 31) & 0x7FFFFFFF), F32)


def _threshold_step(i, key, count_ge, n_sel, bits=1):
    unit = lax.shift_left(jnp.int32(1), 32 - bits * (i + 1))
    digit = jnp.zeros_like(key)
    for j in range(1, 2 ** bits):
        trial = key + j * unit
        ok = jnp.logical_or(count_ge(_key_to_float(trial)) >= n_sel, trial < KEY_NEG_INF)
        digit = digit + jnp.where(ok, 1, 0)
    return key + digit * unit


def _ada_kernel(c_ref, w_ref, b_ref, o_ref):
    o_ref[...] = _dot(c_ref[...].astype(BF16), w_ref[...].astype(BF16)) + b_ref[...]


def _ada(c_all, w_ada, b_ada):
    n, d = c_all.shape
    return pl.pallas_call(
        _ada_kernel,
        grid=(N_ADA,),
        in_specs=[pl.BlockSpec((n, d), lambda j: (0, 0)),
                  pl.BlockSpec((d, d), lambda j: (0, j)),
                  pl.BlockSpec((1, d), lambda j: (0, j))],
        out_specs=pl.BlockSpec((None, n, d), lambda j: (j, 0, 0)),
        out_shape=jax.ShapeDtypeStruct((N_ADA, n, d), F32),
        compiler_params=_cparams(("arbitrary",)),
        name="ada",
    )(c_all, w_ada, b_ada.reshape(1, -1))


def _grouped(x, mod, rows):
    nseq, l, d = x.shape
    if l >= rows:
        assert l % rows == 0
        per = l // rows
        x3 = x.reshape(nseq * per, rows, d)
        mod4 = jnp.broadcast_to(mod[:, :, None, None, :], (N_ADA, nseq, per, 1, d)).reshape(N_ADA, nseq * per, 1, d)
        return x3, mod4, 1
    assert rows % l == 0 and nseq % (rows // l) == 0
    return x, mod[:, :, None, :], rows // l


def _proj_kernel(x_ref, mod_ref, wn_ref, wki_ref, wg_ref, wt_ref,
                 u_ref, kb_ref, kib_ref, kt32_ref, vt32_ref, kit32_ref, vt_ref, qt_ref, qit_ref, wit_ref, ga_ref, gb_ref,
                 *, pw, aw, iw, idim, q_scale):
    ks, r, d = x_ref.shape
    h = (x_ref[...] * (1.0 + mod_ref[1]) + mod_ref[0]).reshape(ks * r, d).astype(BF16)
    u_ref[...] = _dot(h, wn_ref[:, 0:pw])
    kb_ref[...] = _dot(h, wn_ref[:, pw:pw + aw]).astype(BF16)
    kib_ref[...] = _dot(h, wki_ref[...]).astype(BF16)
    ga_ref[...] = _dot(h, wg_ref[:, 0:d])
    gb_ref[...] = _dot(h, wg_ref[:, d:2 * d])
    rows = np.cumsum([0, aw, aw, iw, 2 * SUBLANES, aw, idim])
    part = lambda j: _dot_nt(wt_ref[rows[j]:rows[j + 1], :], h)
    vt = part(0)
    vt32_ref[...] = vt
    vt_ref[...] = vt.astype(BF16)
    qt_ref[...] = (part(1) * q_scale).astype(BF16)
    qit_ref[...] = part(2).astype(BF16)
    wit_ref[...] = part(3)
    kt32_ref[...] = part(4)
    kit32_ref[...] = part(5)


def _proj(x3, mod4, ks, wn, wki, wg, wt, *, pw, aw, iw, idim, q_scale, seq_chunks=None):
    g, r, d = x3.shape
    tm = ks * r
    nb = g // ks
    n = g * r
    row = lambda cols: pl.BlockSpec((tm, cols), lambda i: (i, 0))
    col = lambda rows: pl.BlockSpec((None, rows, tm), lambda i: (i, 0, 0))
    full = lambda a: pl.BlockSpec(a.shape, lambda i: (0,) * a.ndim)
    if seq_chunks is None:
        state = lambda rows: ((nb, rows, tm), F32, col(rows))
    else:
        state = lambda rows: ((nb // seq_chunks, rows, seq_chunks * tm), F32,
                              pl.BlockSpec((None, rows, tm), lambda i: (i // seq_chunks, 0, i % seq_chunks)))
    outs = [((n, pw), F32, row(pw)), ((n, aw), BF16, row(aw)), ((n, idim), BF16, row(idim)),
            state(aw), state(aw), state(idim),
            ((nb, aw, tm), BF16, col(aw)), ((nb, aw, tm), BF16, col(aw)), ((nb, iw, tm), BF16, col(iw)),
            ((nb, 2 * SUBLANES, tm), F32, col(2 * SUBLANES)),
            ((n, d), F32, row(d)), ((n, d), F32, row(d))]
    return pl.pallas_call(
        functools.partial(_proj_kernel, pw=pw, aw=aw, iw=iw, idim=idim, q_scale=q_scale),
        grid=(nb,),
        in_specs=[pl.BlockSpec((ks, r, d), lambda i: (i, 0, 0)),
                  pl.BlockSpec((N_ADA, ks, 1, d), lambda i: (0, i, 0, 0)),
                  full(wn), full(wki), full(wg), full(wt)],
        out_specs=[o[2] for o in outs],
        out_shape=[jax.ShapeDtypeStruct(o[0], o[1]) for o in outs],
        compiler_params=_cparams(("arbitrary",)),
        name="in_proj",
    )(x3, mod4, wn, wki, wg, wt)


def _pool_windows(ext_ref, pos, pw_ref, ps_ref, a_ref):
    ks, rows, width = ext_ref.shape
    r = rows - HIST_ROWS
    gw = width // len(POOL_WINDOWS)
    for g, w in enumerate(POOL_WINDOWS):
        lo = g * gw
        cur = ext_ref[:, HIST_ROWS:HIST_ROWS + r, lo:lo + gw]
        acc = cur
        for j in range(1, w):
            acc = acc + ext_ref[:, HIST_ROWS - j:HIST_ROWS - j + r, lo:lo + gw]
        cnt = jnp.minimum(pos + 1, w).astype(F32)
        pooled = (acc / cnt - cur).reshape(ks * r, gw).astype(BF16)
        mixed = _dot(pooled, pw_ref[g])
        a_ref[:, lo:lo + gw] = (mixed * ps_ref[:, lo:lo + gw]).astype(BF16)


def _pool_prompt_kernel(u_ref, pw_ref, ps_ref, a_ref, ext_ref):
    i = pl.program_id(1)
    tp, width = u_ref.shape
    gw = width // len(POOL_WINDOWS)

    @pl.when(i == 0)
    def _():
        ext_ref[:, 0:HIST_ROWS, :] = jnp.zeros((1, HIST_ROWS, width), F32)

    @pl.when(i > 0)
    def _():
        ext_ref[:, 0:HIST_ROWS, :] = ext_ref[:, tp:tp + HIST_ROWS, :]

    ext_ref[:, HIST_ROWS:, :] = u_ref[...][None]
    pos = i * tp + lax.broadcasted_iota(I32, (1, tp, gw), 1)
    _pool_windows(ext_ref, pos, pw_ref, ps_ref, a_ref)


def _pool_prompt(u, pool_w, pool_scale):
    b, s, width = u.shape
    tp = TOKEN_TILE
    return pl.pallas_call(
        _pool_prompt_kernel,
        grid=(b, s // tp),
        in_specs=[pl.BlockSpec((None, tp, width), lambda bi, i: (bi, i, 0)),
                  pl.BlockSpec(pool_w.shape, lambda bi, i: (0, 0, 0)),
                  pl.BlockSpec(pool_scale.shape, lambda bi, i: (0, 0))],
        out_specs=pl.BlockSpec((None, tp, width), lambda bi, i: (bi, i, 0)),
        out_shape=jax.ShapeDtypeStruct((b, s, width), BF16),
        scratch_shapes=[pltpu.VMEM((1, HIST_ROWS + tp, width), F32)],
        compiler_params=_cparams(("arbitrary", "arbitrary")),
        name="pool_prompt",
    )(u, pool_w, pool_scale)


def _pool_sample_kernel(u_ref, hist_ref, pw_ref, ps_ref, a_ref, ext_ref, *, pos0):
    ks, t, width = u_ref.shape
    gw = width // len(POOL_WINDOWS)
    ext_ref[:, 0:HIST_ROWS, :] = hist_ref[...]
    ext_ref[:, HIST_ROWS:, :] = u_ref[...]
    pos = pos0 + lax.broadcasted_iota(I32, (1, t, gw), 1)
    _pool_windows(ext_ref, pos, pw_ref, ps_ref, a_ref)


def _pool_sample(u, hist16, pool_w, pool_scale, pos0):
    db, t, width = u.shape
    ks = TOKEN_TILE // t
    return pl.pallas_call(
        functools.partial(_pool_sample_kernel, pos0=pos0),
        grid=(db // ks,),
        in_specs=[pl.BlockSpec((ks, t, width), lambda i: (i, 0, 0)),
                  pl.BlockSpec((ks, HIST_ROWS, width), lambda i: (i, 0, 0)),
                  pl.BlockSpec(pool_w.shape, lambda i: (0, 0, 0)),
                  pl.BlockSpec(pool_scale.shape, lambda i: (0, 0))],
        out_specs=pl.BlockSpec((ks * t, width), lambda i: (i, 0)),
        out_shape=jax.ShapeDtypeStruct((db * t, width), BF16),
        scratch_shapes=[pltpu.VMEM((ks, HIST_ROWS + t, width), F32)],
        compiler_params=_cparams(("arbitrary",)),
        name="pool_sample",
    )(u, hist16, pool_w, pool_scale)


def _alibi_slopes(n_heads):
    return [2.0 ** (-8.0 * (h + 1) / n_heads) for h in range(n_heads)]


def _attn_prompt_kernel(qt_ref, qit_ref, wit_ref, kb_ref, vt_ref, kib_ref, o_ref,
                        sc_sc, scb_sc, qz_sc, acc_sc, lg_sc, al_sc, *, n_sel, idx_scale, n_heads, hd, idim, seq_len):
    qb = pl.program_id(1)
    tq = qt_ref.shape[-1]
    sk = tq
    nchunk = qb + 1
    t_idx = qb * tq + lax.broadcasted_iota(I32, (sk, tq), 1)
    s_loc = lax.broadcasted_iota(I32, (sk, tq), 0)

    def score_pair(c2, carry):
        for j in range(2):
            kic = kib_ref[2 * c2 + j]
            for h in range(IDX_HEADS):
                lg_sc[j, h] = _dot(kic, qit_ref[h * idim:(h + 1) * idim, :])
        for j in range(2):
            c = 2 * c2 + j
            acc = jnp.zeros((sk, tq), F32)
            for h in range(IDX_HEADS):
                acc = acc + wit_ref[h:h + 1, :] * jnp.maximum(lg_sc[j, h], 0.0)
            sc = jnp.where(c * sk + s_loc <= t_idx, acc * idx_scale, -jnp.inf)
            sc_sc[c] = sc
            scb_sc[c] = sc.astype(BF16).reshape(sk // BF16_ROWS, BF16_ROWS, tq)
        return carry

    lax.fori_loop(0, (nchunk + 1) // 2, score_pair, 0)

    def colsum(mask):
        return jnp.where(mask, 1.0, 0.0).reshape(sk // SUBLANES, SUBLANES, tq).sum(axis=0)

    def count(pred):
        def pair(c2, a):
            c = 2 * c2
            return a + colsum(pred(c, sc_sc[c])) + colsum(pred(c + 1, sc_sc[c + 1]))
        acc = lax.fori_loop(0, (nchunk + 1) // 2, pair, jnp.zeros((SUBLANES, tq), F32))
        return acc.sum(axis=0, keepdims=True)

    count_ge = lambda t: count(lambda c, s: s >= t)

    def count_ge_rounded(t):
        t_b = jnp.broadcast_to(t, (BF16_ROWS, tq)).astype(BF16)

        def pair(c2, a):
            for j in range(2):
                hit = jnp.where(scb_sc[2 * c2 + j] >= t_b, jnp.ones((), BF16), jnp.zeros((), BF16))
                rows = [hit[g] for g in range(sk // BF16_ROWS)]
                while len(rows) > 1:
                    rows = [x + y for x, y in zip(rows[0::2], rows[1::2])]
                a = a + rows[0].astype(F32)
            return a
        acc = lax.fori_loop(0, (nchunk + 1) // 2, pair, jnp.zeros((BF16_ROWS, tq), F32))
        return acc.sum(axis=0, keepdims=True)

    def wide_key(k16):
        return lax.shift_left(k16, 16) | jnp.where(k16 < 0, 0xFFFF, 0)

    def coarse(i, k16):
        trial = k16 + lax.shift_left(jnp.int32(1), 15 - i)
        k32 = wide_key(trial)
        ok = jnp.logical_or(count_ge_rounded(_key_to_float(k32)) >= n_sel, k32 < KEY_NEG_INF)
        return jnp.where(ok, trial, k16)

    k32 = wide_key(lax.fori_loop(0, 16, coarse, jnp.full((1, tq), -(2 ** 15), I32)))
    half_step, step = 2 ** 15, 2 ** 16

    def fine(i, bracket):
        lo, hi = bracket
        mid = lo + ((hi - lo) >> 1)
        ok = count_ge(_key_to_float(mid)) >= n_sel
        open_ = hi - lo > 1
        return (jnp.where(jnp.logical_and(open_, ok), mid, lo),
                jnp.where(jnp.logical_and(open_, jnp.logical_not(ok)), mid, hi))

    n_fine = int(math.ceil(math.log2(half_step + step)))
    tau_key, _ = lax.fori_loop(0, n_fine, fine, (jnp.maximum(k32 - half_step, KEY_NEG_INF), k32 + step))
    tau = _key_to_float(tau_key)

    tie_rows = jnp.logical_and(count_ge(tau) > n_sel, tau_key > KEY_NEG_INF)
    has_tie = jnp.max(jnp.where(tie_rows, 1.0, 0.0)) > 0.5

    @pl.when(has_tie)
    def _():
        need = n_sel - count(lambda c, s: s > tau)
        nbits = max(1, int(math.ceil(math.log2(seq_len))))

        def first_enough(i, j):
            step = lax.shift_left(jnp.int32(1), nbits - 1 - i)
            upto = count(lambda c, s: jnp.logical_and(s == tau, c * sk + s_loc <= j + step - 1))
            return jnp.where(upto < need, j + step, j)

        last = lax.fori_loop(0, nbits, first_enough, jnp.zeros((1, tq), I32))

        def demote(c, carry):
            s = sc_sc[c]
            drop = jnp.logical_and(jnp.logical_and(tie_rows, s == tau), c * sk + s_loc > last)
            sc_sc[c] = jnp.where(drop, -jnp.inf, s)
            return carry

        lax.fori_loop(0, nchunk, demote, 0)

    pair_rows = lax.broadcasted_iota(I32, (2 * hd, tq), 0)
    for h in range(n_heads):
        blk = qt_ref[(h // 2) * 2 * hd:(h // 2 + 1) * 2 * hd, :]
        keep = pair_rows < hd if h % 2 == 0 else pair_rows >= hd
        qz_sc[h] = jnp.where(keep, blk, jnp.zeros_like(blk))
    acc_sc[...] = jnp.zeros(acc_sc.shape, F32)
    slopes = [s * LOG2E for s in _alibi_slopes(n_heads)]
    for h in range(n_heads):
        al_sc[h] = slopes[h] * s_loc.astype(F32)

    def att_chunk(c, j, m_all, l_all):
        sel = jnp.logical_and(sc_sc[c] >= tau, c * sk + s_loc <= t_idx)
        off = ((c - qb) * sk).astype(F32)
        m_out, l_out = [], []
        for h in range(n_heads):
            lg = jnp.where(sel, lg_sc[j, h] + al_sc[h], -jnp.inf)
            m_new = jnp.maximum(m_all[h], jnp.max(lg, axis=0, keepdims=True) + slopes[h] * off)
            alpha = jnp.exp2(m_all[h] - m_new)
            p = jnp.exp2(lg - (m_new - slopes[h] * off))
            l_out.append(alpha * l_all[h] + jnp.sum(p, axis=0, keepdims=True))
            m_out.append(m_new)
            pv = _dot(vt_ref[c, h * hd:(h + 1) * hd, :], p.astype(BF16))
            acc_sc[h * hd:(h + 1) * hd, :] = alpha * acc_sc[h * hd:(h + 1) * hd, :] + pv
        return tuple(m_out), tuple(l_out)

    def att_pair(c2, carry):
        for j in range(2):
            for h in range(n_heads):
                lg_sc[j, h] = _dot(kb_ref[2 * c2 + j, :, (h // 2) * 2 * hd:(h // 2 + 1) * 2 * hd], qz_sc[h])
        for j in range(2):
            carry = att_chunk(2 * c2 + j, j, *carry)
        return carry

    init = (tuple(jnp.full((1, tq), NEG_BIG, F32) for _ in range(n_heads)),
            tuple(jnp.zeros((1, tq), F32) for _ in range(n_heads)))
    _, l_all = lax.fori_loop(0, (nchunk + 1) // 2, att_pair, init)
    out_t = jnp.concatenate([acc_sc[h * hd:(h + 1) * hd, :] / l_all[h] for h in range(n_heads)], axis=0)
    o_ref[...] = out_t.T.astype(BF16)


def _attn_prompt(qt, qit, wit, kb, vt, kib, *, n_sel, idx_scale, n_heads, hd, idim):
    b, nck, aw, tq = qt.shape
    assert nck % 2 == 0 and IDX_HEADS <= n_heads
    iw = qit.shape[2]
    resident = lambda shape: pl.BlockSpec((None,) + shape, lambda bi, i: (bi, 0, 0, 0), pipeline_mode=pl.Buffered(1))
    tile = lambda rows: pl.BlockSpec((None, None, rows, tq), lambda bi, i: (bi, i, 0, 0))
    return pl.pallas_call(
        functools.partial(_attn_prompt_kernel, n_sel=n_sel, idx_scale=idx_scale, n_heads=n_heads, hd=hd, idim=idim,
                          seq_len=nck * tq),
        grid=(b, nck),
        in_specs=[tile(aw), tile(iw), tile(2 * SUBLANES),
                  resident((nck, tq, aw)), resident((nck, aw, tq)), resident((nck, tq, idim))],
        out_specs=pl.BlockSpec((None, tq, aw), lambda bi, i: (bi, i, 0)),
        out_shape=jax.ShapeDtypeStruct((b, nck * tq, aw), BF16),
        scratch_shapes=[pltpu.VMEM((nck, tq, tq), F32),
                        pltpu.VMEM((nck, tq // BF16_ROWS, BF16_ROWS, tq), BF16),
                        pltpu.VMEM((n_heads, 2 * hd, tq), BF16),
                        pltpu.VMEM((aw, tq), F32),
                        pltpu.VMEM((2, n_heads, tq, tq), F32),
                        pltpu.VMEM((n_heads, tq, tq), F32)],
        compiler_params=_cparams(("arbitrary", "arbitrary")),
        name="attn_prompt",
    )(qt, qit, wit, kb, vt, kib)


def _attn_sample_kernel(pt_ref, qbd_ref, qi_ref, wi_ref, knew_ref, vnew_ref, kinew_ref, *rest,
                        n_pages, page, n_sel, idx_scale, n_heads, hd, t_new):
    del pt_ref
    ki_refs, k_refs, v_refs = rest[0:n_pages], rest[n_pages:2 * n_pages], rest[2 * n_pages:3 * n_pages]
    o_ref, sc_sc, lg_sc = rest[3 * n_pages:]
    past = n_pages * page
    width = past + page
    group = n_heads // 2
    row = lax.broadcasted_iota(I32, (t_new, page), 0)
    lane = lax.broadcasted_iota(I32, (t_new, page), 1)
    wi = wi_ref[...]

    def scores(kit):
        rel = jnp.maximum(_dot(qi_ref[...], kit), 0.0)
        return jnp.sum(rel.reshape(IDX_HEADS, t_new, rel.shape[1]) * wi, axis=0) * idx_scale

    def page_pair(refs, pp, head_lo=None):
        if head_lo is None:
            parts = [refs[2 * pp + j][...] for j in range(2)]
        else:
            parts = [refs[2 * pp + j][head_lo:head_lo + group].reshape(group * hd, page) for j in range(2)]
        return jnp.concatenate(parts, axis=1).astype(BF16)

    for pp in range(n_pages // 2):
        sc_sc[:, 2 * pp * page:(2 * pp + 2) * page] = scores(page_pair(ki_refs, pp))
    sc_sc[:, past:width] = jnp.where(lane <= row, scores(kinew_ref[...]), -jnp.inf)

    def count(pred):
        return jnp.sum(jnp.where(pred(sc_sc[...]), 1.0, 0.0), axis=1, keepdims=True)

    count_ge = lambda t: count(lambda s: s >= t)
    tau_key = lax.fori_loop(0, 32 // SAMPLE_SEARCH_BITS,
                            lambda i, k: _threshold_step(i, k, count_ge, n_sel, SAMPLE_SEARCH_BITS),
                            jnp.full((t_new, 1), INT_MIN, I32))
    tau = _key_to_float(tau_key)

    tie_rows = jnp.logical_and(count_ge(tau) > n_sel, tau_key > KEY_NEG_INF)
    has_tie = jnp.max(jnp.where(tie_rows, 1.0, 0.0)) > 0.5
    pos = lax.broadcasted_iota(I32, (t_new, width), 1)

    @pl.when(has_tie)
    def _():
        need = n_sel - count(lambda s: s > tau)
        nbits = max(1, int(math.ceil(math.log2(width))))

        def first_enough(i, j):
            step = lax.shift_left(jnp.int32(1), nbits - 1 - i)
            upto = count(lambda s: jnp.logical_and(s == tau, pos <= j + step - 1))
            return jnp.where(upto < need, j + step, j)

        last = lax.fori_loop(0, nbits, first_enough, jnp.zeros((t_new, 1), I32))
        s = sc_sc[...]
        drop = jnp.logical_and(jnp.logical_and(tie_rows, s == tau), pos > last)
        sc_sc[...] = jnp.where(drop, -jnp.inf, s)

    slopes = [s * LOG2E for s in _alibi_slopes(n_heads)]
    t_pos = past + lax.broadcasted_iota(I32, (t_new, width), 0)
    sel = jnp.logical_and(sc_sc[...] >= tau, pos <= t_pos)
    bias = jnp.where(sel, 0.0, -jnp.inf)
    dist = (t_pos - pos).astype(F32)
    spans = [(2 * pp * page, 2 * page) for pp in range(n_pages // 2)] + [(past, page)]
    for g in range(2):
        for pp, (lo, n) in enumerate(spans):
            kt = page_pair(k_refs, pp, g * group) if pp < n_pages // 2 else knew_ref[g]
            lg = _dot(qbd_ref[g], kt).reshape(group, t_new, n)
            for hl in range(group):
                h = g * group + hl
                lg_sc[h, :, lo:lo + n] = lg[hl] - slopes[h] * dist[:, lo:lo + n] + bias[:, lo:lo + n]
    for g in range(2):
        probs, denoms = [], []
        for hl in range(group):
            lg = lg_sc[g * group + hl]
            pr = jnp.exp2(lg - jnp.max(lg, axis=1, keepdims=True))
            denoms.append(jnp.sum(pr, axis=1, keepdims=True))
            probs.append(pr)
        pg = jnp.concatenate(probs, axis=0).astype(BF16)
        acc = jnp.zeros((group * t_new, group * hd), F32)
        for pp, (lo, n) in enumerate(spans):
            vt = page_pair(v_refs, pp, g * group) if pp < n_pages // 2 else vnew_ref[g]
            acc = acc + _dot_nt(pg[:, lo:lo + n], vt)
        for hl in range(group):
            o_ref[g * group + hl] = acc[hl * t_new:(hl + 1) * t_new, hl * hd:(hl + 1) * hd] / denoms[hl]


def _attn_sample(page_table, qbd, qi2, wi3, knew, vnew, kinew, k_t, v_t, ki_t, layer, *, n_sel, idx_scale):
    db = qbd.shape[0]
    n_pages = page_table.shape[1]
    n_heads, hd, page = k_t.shape[2:]
    idim = ki_t.shape[2]
    t_new = wi3.shape[2]
    assert n_pages % 2 == 0 and n_heads % 2 == 0
    per_seq = lambda a: pl.BlockSpec((None,) + a.shape[1:], lambda b, pt: (b,) + (0,) * (a.ndim - 1))
    kv_page = lambda j: pl.BlockSpec((None, None, n_heads, hd, page), lambda b, pt: (layer, pt[b, j], 0, 0, 0))
    ki_page = lambda j: pl.BlockSpec((None, None, idim, page), lambda b, pt: (layer, pt[b, j], 0, 0))
    in_specs = ([per_seq(qbd), per_seq(qi2), per_seq(wi3), per_seq(knew), per_seq(vnew), per_seq(kinew)]
                + [ki_page(j) for j in range(n_pages)]
                + [kv_page(j) for j in range(n_pages)]
                + [kv_page(j) for j in range(n_pages)])
    grid_spec = pltpu.PrefetchScalarGridSpec(
        num_scalar_prefetch=1,
        grid=(db,),
        in_specs=in_specs,
        out_specs=pl.BlockSpec((None, n_heads, t_new, hd), lambda b, pt: (b, 0, 0, 0)),
        scratch_shapes=[pltpu.VMEM((t_new, (n_pages + 1) * page), F32),
                        pltpu.VMEM((n_heads, t_new, (n_pages + 1) * page), F32)],
    )
    return pl.pallas_call(
        functools.partial(_attn_sample_kernel, n_pages=n_pages, page=page, n_sel=n_sel, idx_scale=idx_scale,
                          n_heads=n_heads, hd=hd, t_new=t_new),
        grid_spec=grid_spec,
        out_shape=jax.ShapeDtypeStruct((db, n_heads, t_new, hd), F32),
        compiler_params=_cparams(("arbitrary",)),
        name="attn_sample",
    )(page_table, qbd, qi2, wi3, knew, vnew, kinew, *([ki_t] * n_pages), *([k_t] * n_pages), *([v_t] * n_pages))


def _layer_norm(y, g, b):
    mu = jnp.mean(y, axis=-1, keepdims=True)
    var = jnp.mean(jnp.square(y - mu), axis=-1, keepdims=True)
    return (y - mu) * lax.rsqrt(var + LN_EPS) * g + b


def _merge_kernel(x_ref, mod_ref, a_ref, o_ref, ga_ref, gb_ref, wa_ref, wb_ref, wo_ref, g_ref, b_ref, x1_ref, *, alpha):
    ks, r, d = x_ref.shape
    merged = (jax.nn.sigmoid(ga_ref[...]) * _dot(a_ref[...], wa_ref[...])
              + jax.nn.sigmoid(gb_ref[...]) * _dot(o_ref[...], wb_ref[...]))
    upd = _dot(merged.astype(BF16), wo_ref[...]).reshape(ks, r, d)
    y = alpha * x_ref[...] + mod_ref[2] * upd
    x1_ref[...] = _layer_norm(y, g_ref[...], b_ref[...])


def _merge(x3, mod4, ks, a, o, ga, gb, wa, wb, wo, ln_g, ln_b, alpha):
    g, r, d = x3.shape
    tm = ks * r
    row = lambda arr: pl.BlockSpec((tm, arr.shape[1]), lambda i: (i, 0))
    full = lambda arr: pl.BlockSpec(arr.shape, lambda i: (0,) * arr.ndim)
    return pl.pallas_call(
        functools.partial(_merge_kernel, alpha=alpha),
        grid=(g // ks,),
        in_specs=[pl.BlockSpec((ks, r, d), lambda i: (i, 0, 0)),
                  pl.BlockSpec((N_ADA, ks, 1, d), lambda i: (0, i, 0, 0)),
                  row(a), row(o), row(ga), row(gb), full(wa), full(wb), full(wo), full(ln_g), full(ln_b)],
        out_specs=pl.BlockSpec((ks, r, d), lambda i: (i, 0, 0)),
        out_shape=jax.ShapeDtypeStruct((g, r, d), F32),
        compiler_params=_cparams(("arbitrary",)),
        name="merge_ln1",
    )(x3, mod4, a, o, ga, gb, wa, wb, wo, ln_g, ln_b)


def _top_rows_exact(s, n_top):
    rows = s.shape[0]
    row_id = lax.broadcasted_iota(I32, s.shape, 0)
    rank = jnp.full(s.shape, n_top, I32)
    vals = []
    for r in range(n_top):
        m = jnp.max(s, axis=0, keepdims=True)
        idx = jnp.min(jnp.where(s == m, row_id, rows), axis=0, keepdims=True)
        hit = row_id == idx
        vals.append(m)
        rank = jnp.where(hit, r, rank)
        s = jnp.where(hit, -jnp.inf, s)
    return jnp.concatenate(vals, axis=0), rank


def _top_rows_distinct(s, n_top):
    rank = jnp.full(s.shape, n_top, I32)
    vals = []
    for r in range(n_top):
        m = jnp.max(s, axis=0, keepdims=True)
        hit = s == m
        vals.append(m)
        rank = jnp.where(hit, r, rank)
        s = jnp.where(hit, -jnp.inf, s)
    return jnp.concatenate(vals, axis=0), rank


def _rank_ties(rank, n_top):
    taken = jnp.sum(jnp.where(rank < n_top, 1.0, 0.0), axis=0, keepdims=True)
    return jnp.max(jnp.where(taken != n_top, 1.0, 0.0)) > 0.5


PEER_PAIRS = [(a, b) for a in range(PEER_TOPK) for b in range(PEER_TOPK) if (a + 1) * (b + 1) <= PEER_TOPK]
PEER_CAND_ROWS = -(-len(PEER_PAIRS) // SUBLANES) * SUBLANES


def _peer_kernel(x_ref, mod_ref, wq_ref, kt_ref, u_ref, vt_ref, g_ref, b_ref, y_ref,
                 h2_sc, st_sc, e0_sc, nb_sc, e1_sc, r1_sc, acc_sc, sv_sc, rk_sc, ts_sc, r2_sc, act_sc, w_sc,
                 *, alpha, n_keys):
    ks, r, d = x_ref.shape
    tm = ks * r
    step = pl.program_id(1)
    sub = PEER_EXPERT_TILE // n_keys

    @pl.when(step == 0)
    def _():
        h2 = (x_ref[...] * (1.0 + mod_ref[4]) + mod_ref[3]).reshape(tm, d).astype(BF16)
        h2_sc[...] = h2
        qp = _dot(h2, wq_ref[...]).astype(BF16)
        st_sc[...] = _dot_nt(kt_ref[...], qp)
        acc_sc[...] = jnp.zeros(acc_sc.shape, F32)

        def route(h, carry):
            s0 = st_sc[pl.ds(pl.multiple_of(h * 2 * n_keys, 2 * n_keys), n_keys), :]
            s1 = st_sc[pl.ds(pl.multiple_of(h * 2 * n_keys + n_keys, n_keys), n_keys), :]

            def halves(top):
                sv_sc[0], rk_sc[0] = top(s0, PEER_TOPK)
                sv_sc[1], rk_sc[1] = top(s1, PEER_TOPK)

            halves(_top_rows_distinct)

            @pl.when(jnp.logical_or(_rank_ties(rk_sc[0], PEER_TOPK), _rank_ties(rk_sc[1], PEER_TOPK)))
            def _():
                halves(_top_rows_exact)

            v0, v1, rank0 = sv_sc[0], sv_sc[1], rk_sc[0]
            pad = [jnp.full((PEER_CAND_ROWS - len(PEER_PAIRS), tm), -jnp.inf, F32)]
            cand = jnp.concatenate([v0[a:a + 1] + v1[b:b + 1] for a, b in PEER_PAIRS] + pad, axis=0)

            def pairs(top):
                ts_sc[...], r2_sc[...] = top(cand, PEER_TOPK)

            pairs(_top_rows_distinct)

            @pl.when(_rank_ties(r2_sc[...], PEER_TOPK))
            def _():
                pairs(_top_rows_exact)

            top_s, rank2 = ts_sc[...], r2_sc[...]
            z = jnp.sum(jnp.exp(top_s - top_s[0:1]), axis=0, keepdims=True)
            took = jnp.where(rank2 < PEER_TOPK, 1, 0)
            take = jnp.zeros(rank0.shape, I32)
            for a in range(PEER_TOPK):
                n_a = sum(took[i:i + 1] for i, (pa, _) in enumerate(PEER_PAIRS) if pa == a)
                take = jnp.where(rank0 == a, n_a, take)
            e0_sc[h] = (jnp.exp(s0 - v0[0:1]) / z * 0.5).reshape(n_keys // SUBLANES, SUBLANES, tm)
            nb_sc[h] = take.astype(F32).reshape(n_keys // SUBLANES, SUBLANES, tm)
            e1_sc[h] = jnp.exp(s1 - v1[0:1]).astype(BF16).reshape(n_keys // BF16_ROWS, BF16_ROWS, tm)
            r1_sc[h] = rk_sc[1].astype(F32).astype(BF16).reshape(n_keys // BF16_ROWS, BF16_ROWS, tm)
            return carry

        lax.fori_loop(0, PEER_HEADS, route, 0)

    n_tok = tm // PEER_TOKEN_SPLIT
    for t in range(PEER_TOKEN_SPLIT):
        act_sc[t] = _dot_nt(u_ref[...], h2_sc[t * n_tok:(t + 1) * n_tok, :])
    for t in range(PEER_TOKEN_SPLIT):
        lanes = slice(t * n_tok, (t + 1) * n_tok)
        for ii in range(sub):
            act = act_sc[t, ii * n_keys:(ii + 1) * n_keys, :]
            gate = jnp.zeros((n_keys // BF16_ROWS, BF16_ROWS, n_tok), BF16)
            for h in range(PEER_HEADS):
                take_i = jnp.broadcast_to(nb_sc[h, step, ii:ii + 1, lanes], (BF16_ROWS, n_tok)).astype(BF16)
                e0_i = jnp.broadcast_to(e0_sc[h, step, ii:ii + 1, lanes], (BF16_ROWS, n_tok)).astype(BF16)
                e1 = e1_sc[h, :, :, lanes]
                gate = gate + jnp.where(r1_sc[h, :, :, lanes] < take_i, e1, jnp.zeros_like(e1)) * e0_i
            gelu2 = act * (1.0 + lax.erf(act * (2.0 ** -0.5)))
            w = gelu2.astype(BF16).reshape(n_keys // BF16_ROWS, BF16_ROWS, n_tok) * gate
            w_sc[t, ii * n_keys:(ii + 1) * n_keys, :] = w.reshape(n_keys, n_tok)
    for t in range(PEER_TOKEN_SPLIT):
        acc_sc[:, t * n_tok:(t + 1) * n_tok] += _dot(vt_ref[...], w_sc[t])

    @pl.when(step == pl.num_programs(1) - 1)
    def _():
        f = acc_sc[...].T.reshape(ks, r, d)
        y = alpha * x_ref[...] + mod_ref[5] * f
        y_ref[...] = _layer_norm(y, g_ref[...], b_ref[...])


def _peer(x3, mod4, ks, wq, kt, u_b, vt_b, ln_g, ln_b, alpha, n_keys):
    g, r, d = x3.shape
    tm = ks * r
    n_exp = u_b.shape[0]
    assert PEER_EXPERT_TILE == SUBLANES * n_keys and n_exp % PEER_EXPERT_TILE == 0
    full = lambda arr: pl.BlockSpec(arr.shape, lambda i, e: (0,) * arr.ndim)
    return pl.pallas_call(
        functools.partial(_peer_kernel, alpha=alpha, n_keys=n_keys),
        grid=(g // ks, n_exp // PEER_EXPERT_TILE),
        in_specs=[pl.BlockSpec((ks, r, d), lambda i, e: (i, 0, 0)),
                  pl.BlockSpec((N_ADA, ks, 1, d), lambda i, e: (0, i, 0, 0)),
                  full(wq), full(kt),
                  pl.BlockSpec((PEER_EXPERT_TILE, d), lambda i, e: (e, 0)),
                  pl.BlockSpec((d, PEER_EXPERT_TILE), lambda i, e: (0, e)),
                  full(ln_g), full(ln_b)],
        out_specs=pl.BlockSpec((ks, r, d), lambda i, e: (i, 0, 0)),
        out_shape=jax.ShapeDtypeStruct((g, r, d), F32),
        scratch_shapes=[pltpu.VMEM((tm, d), BF16),
                        pltpu.VMEM((kt.shape[0], tm), F32),
                        pltpu.VMEM((PEER_HEADS, n_keys // SUBLANES, SUBLANES, tm), F32),
                        pltpu.VMEM((PEER_HEADS, n_keys // SUBLANES, SUBLANES, tm), F32),
                        pltpu.VMEM((PEER_HEADS, n_keys // BF16_ROWS, BF16_ROWS, tm), BF16),
                        pltpu.VMEM((PEER_HEADS, n_keys // BF16_ROWS, BF16_ROWS, tm), BF16),
                        pltpu.VMEM((d, tm), F32),
                        pltpu.VMEM((2, PEER_TOPK, tm), F32),
                        pltpu.VMEM((2, n_keys, tm), I32),
                        pltpu.VMEM((PEER_TOPK, tm), F32),
                        pltpu.VMEM((PEER_CAND_ROWS, tm), I32),
                        pltpu.VMEM((PEER_TOKEN_SPLIT, PEER_EXPERT_TILE, tm // PEER_TOKEN_SPLIT), F32),
                        pltpu.VMEM((PEER_TOKEN_SPLIT, PEER_EXPERT_TILE, tm // PEER_TOKEN_SPLIT), BF16)],
        compiler_params=_cparams(("arbitrary", "arbitrary")),
        name="peer_ln2",
    )(x3, mod4, wq, kt, u_b, vt_b, ln_g, ln_b)


def _layer_weights(w_in, pool_w, w_branch_a, w_branch_b, w_out, peer_wq, peer_sub_keys, peer_u, peer_v,
                   *, pw, aw, iw, idim):
    d = w_in.shape[0]
    o = np.cumsum([0, pw, aw, aw, aw, iw, idim, IDX_HEADS, d, d])
    cut = lambda j: w_in[:, o[j]:o[j + 1]]
    u_w, q_w, k_w, v_w, qi_w, ki_w, wi_w, ga_w, gb_w = (cut(j) for j in range(9))
    wn = jnp.concatenate([u_w, k_w], axis=1).astype(BF16)
    wg = jnp.concatenate([ga_w, gb_w], axis=1).astype(BF16)
    wi_pad = jnp.pad(wi_w, ((0, 0), (0, 2 * SUBLANES - IDX_HEADS)))
    wt = jnp.concatenate([v_w, q_w, qi_w, wi_pad, k_w, ki_w], axis=1).T.astype(BF16)
    n_keys, half = peer_sub_keys.shape[1], peer_sub_keys.shape[2]
    kt = jnp.einsum("hg,cb,ckd->hckgbd", jnp.eye(PEER_HEADS, dtype=F32), jnp.eye(2, dtype=F32), peer_sub_keys)
    kt = kt.reshape(PEER_HEADS * 2 * n_keys, PEER_HEADS * 2 * half).astype(BF16)
    return dict(wn=wn, wki=ki_w.astype(BF16), wg=wg, wt=wt, pool_w=pool_w.astype(BF16),
                wa=w_branch_a.astype(BF16), wb=w_branch_b.astype(BF16), wo=w_out.astype(BF16),
                wq=peer_wq.astype(BF16), kt=kt, u_b=peer_u.astype(BF16), vt_b=peer_v.T.astype(BF16))


def kernel(x_prompt, x_sample, cache_k, cache_v, cache_kidx, state_pool, page_table, c_prompt, c_sample,
           w_ada, b_ada, w_in, pool_w, pool_scale, w_branch_a, w_branch_b, w_out, ln1_g, ln1_b,
           peer_wq, peer_sub_keys, peer_u, peer_v, ln2_g, ln2_b):
    depth = w_ada.shape[0]
    b, s, d = x_prompt.shape
    db, t_new, _ = x_sample.shape
    page, n_heads, hd = cache_k.shape[2:]
    idim = cache_kidx.shape[3]
    pool_hist, pw = state_pool.shape[2:]
    n_pages = page_table.shape[1]
    past = n_pages * page
    aw, iw = n_heads * hd, IDX_HEADS * idim
    n_keys = peer_sub_keys.shape[2]
    alpha = (2 * depth) ** 0.25
    idx_scale = float(iw) ** -0.5
    q_scale = float(hd) ** -0.5 * LOG2E
    assert t_new == SUBLANES and n_heads == SUBLANES and pool_hist < HIST_ROWS and s % TOKEN_TILE == 0

    n_c = b + db
    c_all = jnp.pad(jnp.concatenate([c_prompt, c_sample], axis=0), ((0, -n_c % SUBLANES), (0, 0)))
    y_p, y_s = x_prompt, x_sample
    k_t = jnp.transpose(cache_k, (0, 1, 3, 4, 2))
    v_t = jnp.transpose(cache_v, (0, 1, 3, 4, 2))
    ki_t = jnp.transpose(cache_kidx, (0, 1, 3, 2))
    outs = [[] for _ in range(8)]
    for l in range(depth):
        w = _layer_weights(w_in[l], pool_w[l], w_branch_a[l], w_branch_b[l], w_out[l], peer_wq[l], peer_sub_keys[l],
                           peer_u[l], peer_v[l], pw=pw, aw=aw, iw=iw, idim=idim)
        mod = _ada(c_all, w_ada[l], b_ada[l])
        mod_p, mod_s = mod[:, :b], mod[:, b:n_c]
        ps, g1, b1, g2, b2 = pool_scale[l][None], ln1_g[l][None], ln1_b[l][None], ln2_g[l][None], ln2_b[l][None]
        proj = functools.partial(_proj, wn=w["wn"], wki=w["wki"], wg=w["wg"], wt=w["wt"],
                                 pw=pw, aw=aw, iw=iw, idim=idim, q_scale=q_scale)

        x3, mod4, ks = _grouped(y_p, mod_p, TOKEN_TILE)
        nck = s // TOKEN_TILE
        u, kb, kib, kt32, vt32, kit32, vt, qt, qit, wit, ga, gb = proj(x3, mod4, ks, seq_chunks=nck)
        a = _pool_prompt(u.reshape(b, s, pw), w["pool_w"], ps)
        o = _attn_prompt(qt.reshape(b, nck, aw, TOKEN_TILE), qit.reshape(b, nck, iw, TOKEN_TILE),
                         wit.reshape(b, nck, 2 * SUBLANES, TOKEN_TILE), kb.reshape(b, nck, TOKEN_TILE, aw),
                         vt.reshape(b, nck, aw, TOKEN_TILE), kib.reshape(b, nck, TOKEN_TILE, idim),
                         n_sel=min(TOPK_MAX, s // 4), idx_scale=idx_scale, n_heads=n_heads, hd=hd, idim=idim)
        x1 = _merge(x3, mod4, ks, a.reshape(b * s, pw), o.reshape(b * s, aw), ga, gb,
                    w["wa"], w["wb"], w["wo"], g1, b1, alpha)
        x3p, mod4p, ksp = _grouped(x1.reshape(b, s, d), mod_p, PEER_TOKEN_TILE)
        y_p = _peer(x3p, mod4p, ksp, w["wq"], w["kt"], w["u_b"], w["vt_b"], g2, b2, alpha, n_keys).reshape(b, s, d)
        heads_last = lambda z: jnp.transpose(z.reshape(z.shape[0], n_heads, hd, z.shape[2]), (0, 3, 1, 2))
        outs[0].append(heads_last(kt32))
        outs[1].append(heads_last(vt32))
        outs[2].append(jnp.transpose(kit32, (0, 2, 1)))
        outs[3].append(u.reshape(b, s, pw)[:, s - pool_hist:])

        x3, mod4, ks = _grouped(y_s, mod_s, TOKEN_TILE)
        u, kb, kib, kt32, vt32, kit32, vt, qt, qit, wit, ga, gb = proj(x3, mod4, ks)
        hist = state_pool[l]
        hist16 = jnp.pad(hist, ((0, 0), (HIST_ROWS - pool_hist, 0), (0, 0)))
        a = _pool_sample(u.reshape(db, t_new, pw), hist16, w["pool_w"], ps, past)
        untile = lambda z: jnp.moveaxis(z, 0, 1).reshape(z.shape[1], db, t_new)
        group = n_heads // 2
        q_s = jnp.transpose(untile(qt).reshape(2, group, hd, db, t_new), (3, 0, 1, 4, 2))
        qbd = jnp.einsum("bgltd,lm->bgltmd", q_s, jnp.eye(group, dtype=BF16)).reshape(db, 2, group * t_new, group * hd)
        qi2 = jnp.transpose(untile(qit).reshape(IDX_HEADS, idim, db, t_new), (2, 0, 3, 1)).reshape(db, IDX_HEADS * t_new, idim)
        wi3 = jnp.transpose(untile(wit)[:IDX_HEADS], (1, 0, 2))[..., None]
        lane_pad = lambda z: jnp.pad(z, ((0, 0),) * (z.ndim - 1) + ((0, page - t_new),))
        knew = lane_pad(jnp.transpose(kb.reshape(db, t_new, 2, group * hd), (0, 2, 3, 1)))
        vnew = lane_pad(jnp.transpose(untile(vt).reshape(2, group * hd, db, t_new), (2, 0, 1, 3)))
        kinew = lane_pad(jnp.transpose(kib.reshape(db, t_new, idim), (0, 2, 1)))
        o = _attn_sample(page_table, qbd, qi2, wi3, knew, vnew, kinew, k_t, v_t, ki_t, l,
                         n_sel=min(TOPK_MAX, (past + t_new) // 4), idx_scale=idx_scale)
        o = jnp.transpose(o, (0, 2, 1, 3)).reshape(db * t_new, aw).astype(BF16)
        x1 = _merge(x3, mod4, ks, a, o, ga, gb, w["wa"], w["wb"], w["wo"], g1, b1, alpha)
        x3p, mod4p, ksp = _grouped(x1, mod_s, PEER_TOKEN_TILE)
        y_s = _peer(x3p, mod4p, ksp, w["wq"], w["kt"], w["u_b"], w["vt_b"], g2, b2, alpha, n_keys)
        tokens_first = lambda z: jnp.transpose(untile(z), (1, 2, 0))
        outs[4].append(tokens_first(kt32).reshape(db, t_new, n_heads, hd))
        outs[5].append(tokens_first(vt32).reshape(db, t_new, n_heads, hd))
        outs[6].append(tokens_first(kit32))
        outs[7].append(jnp.concatenate([hist, u.reshape(db, t_new, pw)], axis=1)[:, t_new:])
    return (y_p, y_s) + tuple(jnp.stack(o_) for o_ in outs)
```

```python
import functools
import math

import jax
import jax.numpy as jnp
import numpy as np
from jax import lax
from jax.experimental import pallas as pl
from jax.experimental.pallas import tpu as pltpu

F32, BF16, I32 = jnp.float32, jnp.bfloat16, jnp.int32

POOL_WINDOWS = (2, 4, 8, 16)
IDX_HEADS = 8
TOPK_MAX = 256
PEER_HEADS = 8
PEER_TOPK = 16
LN_EPS = 1e-5
N_ADA = 6

VMEM_LIMIT_BYTES = 56 * 1024 * 1024
SUBLANES = 8
LANES = 128
BF16_ROWS = 16
TOKEN_TILE = 256
PEER_TOKEN_TILE = 512
PEER_EXPERT_TILE = 1024
PEER_TOKEN_SPLIT = 2
SAMPLE_SEARCH_BITS = 4
HIST_ROWS = 16

INT_MIN = -(2 ** 31)
KEY_NEG_INF = -2139095041
NEG_BIG = -1e30
LOG2E = 1.4426950408889634


def _cparams(sem):
    return pltpu.CompilerParams(dimension_semantics=sem, vmem_limit_bytes=VMEM_LIMIT_BYTES)


def _dot(a, b):
    return jnp.dot(a, b, preferred_element_type=F32)


def _dot_nt(a, b):
    return lax.dot_general(a, b, (((1,), (1,)), ((), ())), preferred_element_type=F32)


def _key_to_float(key):
    return lax.bitcast_convert_type(key ^ ((key >> 31) & 0x7FFFFFFF), F32)


def _threshold_step(i, key, count_ge, n_sel, bits=1):
    unit = lax.shift_left(jnp.int32(1), 32 - bits * (i + 1))
    digit = jnp.zeros_like(key)
    for j in range(1, 2 ** bits):
        trial = key + j * unit
        ok = jnp.logical_or(count_ge(_key_to_float(trial)) >= n_sel, trial < KEY_NEG_INF)
        digit = digit + jnp.where(ok, 1, 0)
    return key + digit * unit


def _ada_kernel(c_ref, w_ref, b_ref, o_ref):
    o_ref[...] = _dot(c_ref[...].astype(BF16), w_ref[...].astype(BF16)) + b_ref[...]


def _ada(c_all, w_ada, b_ada):
    n, d = c_all.shape
    return pl.pallas_call(
        _ada_kernel,
        grid=(N_ADA,),
        in_specs=[pl.BlockSpec((n, d), lambda j: (0, 0)),
                  pl.BlockSpec((d, d), lambda j: (0, j)),
                  pl.BlockSpec((1, d), lambda j: (0, j))],
        out_specs=pl.BlockSpec((None, n, d), lambda j: (j, 0, 0)),
        out_shape=jax.ShapeDtypeStruct((N_ADA, n, d), F32),
        compiler_params=_cparams(("arbitrary",)),
        name="ada",
    )(c_all, w_ada, b_ada.reshape(1, -1))


def _grouped(x, mod, rows):
    nseq, l, d = x.shape
    if l >= rows:
        assert l % rows == 0
        per = l // rows
        x3 = x.reshape(nseq * per, rows, d)
        mod4 = jnp.broadcast_to(mod[:, :, None, None, :], (N_ADA, nseq, per, 1, d)).reshape(N_ADA, nseq * per, 1, d)
        return x3, mod4, 1
    assert rows % l == 0 and nseq % (rows // l) == 0
    return x, mod[:, :, None, :], rows // l


def _proj_kernel(x_ref, mod_ref, wn_ref, wki_ref, wg_ref, wt_ref,
                 u_ref, kb_ref, kib_ref, kt32_ref, vt32_ref, kit32_ref, vt_ref, qt_ref, qit_ref, wit_ref, ga_ref, gb_ref,
                 *, pw, aw, iw, idim, q_scale):
    ks, r, d = x_ref.shape
    h = (x_ref[...] * (1.0 + mod_ref[1]) + mod_ref[0]).reshape(ks * r, d).astype(BF16)
    u_ref[...] = _dot(h, wn_ref[:, 0:pw])
    kb_ref[...] = _dot(h, wn_ref[:, pw:pw + aw]).astype(BF16)
    kib_ref[...] = _dot(h, wki_ref[...]).astype(BF16)
    ga_ref[...] = _dot(h, wg_ref[:, 0:d])
    gb_ref[...] = _dot(h, wg_ref[:, d:2 * d])
    rows = np.cumsum([0, aw, aw, iw, 2 * SUBLANES, aw, idim])
    part = lambda j: _dot_nt(wt_ref[rows[j]:rows[j + 1], :], h)
    vt = part(0)
    vt32_ref[...] = vt
    vt_ref[...] = vt.astype(BF16)
    qt_ref[...] = (part(1) * q_scale).astype(BF16)
    qit_ref[...] = part(2).astype(BF16)
    wit_ref[...] = part(3)
    kt32_ref[...] = part(4)
    kit32_ref[...] = part(5)


def _proj(x3, mod4, ks, wn, wki, wg, wt, *, pw, aw, iw, idim, q_scale, seq_chunks=None):
    g, r, d = x3.shape
    tm = ks * r
    nb = g // ks
    n = g * r
    row = lambda cols: pl.BlockSpec((tm, cols), lambda i: (i, 0))
    col = lambda rows: pl.BlockSpec((None, rows, tm), lambda i: (i, 0, 0))
    full = lambda a: pl.BlockSpec(a.shape, lambda i: (0,) * a.ndim)
    if seq_chunks is None:
        state = lambda rows: ((nb, rows, tm), F32, col(rows))
    else:
        state = lambda rows: ((nb // seq_chunks, rows, seq_chunks * tm), F32,
                              pl.BlockSpec((None, rows, tm), lambda i: (i // seq_chunks, 0, i % seq_chunks)))
    outs = [((n, pw), F32, row(pw)), ((n, aw), BF16, row(aw)), ((n, idim), BF16, row(idim)),
            state(aw), state(aw), state(idim),
            ((nb, aw, tm), BF16, col(aw)), ((nb, aw, tm), BF16, col(aw)), ((nb, iw, tm), BF16, col(iw)),
            ((nb, 2 * SUBLANES, tm), F32, col(2 * SUBLANES)),
            ((n, d), F32, row(d)), ((n, d), F32, row(d))]
    return pl.pallas_call(
        functools.partial(_proj_kernel, pw=pw, aw=aw, iw=iw, idim=idim, q_scale=q_scale),
        grid=(nb,),
        in_specs=[pl.BlockSpec((ks, r, d), lambda i: (i, 0, 0)),
                  pl.BlockSpec((N_ADA, ks, 1, d), lambda i: (0, i, 0, 0)),
                  full(wn), full(wki), full(wg), full(wt)],
        out_specs=[o[2] for o in outs],
        out_shape=[jax.ShapeDtypeStruct(o[0], o[1]) for o in outs],
        compiler_params=_cparams(("arbitrary",)),
        name="in_proj",
    )(x3, mod4, wn, wki, wg, wt)


def _pool_windows(ext_ref, pos, pw_ref, ps_ref, a_ref):
    ks, rows, width = ext_ref.shape
    r = rows - HIST_ROWS
    gw = width // len(POOL_WINDOWS)
    for g, w in enumerate(POOL_WINDOWS):
        lo = g * gw
        cur = ext_ref[:, HIST_ROWS:HIST_ROWS + r, lo:lo + gw]
        acc = cur
        for j in range(1, w):
            acc = acc + ext_ref[:, HIST_ROWS - j:HIST_ROWS - j + r, lo:lo + gw]
        cnt = jnp.minimum(pos + 1, w).astype(F32)
        pooled = (acc / cnt - cur).reshape(ks * r, gw).astype(BF16)
        mixed = _dot(pooled, pw_ref[g])
        a_ref[:, lo:lo + gw] = (mixed * ps_ref[:, lo:lo + gw]).astype(BF16)


def _pool_prompt_kernel(u_ref, pw_ref, ps_ref, a_ref, ext_ref):
    i = pl.program_id(1)
    tp, width = u_ref.shape
    gw = width // len(POOL_WINDOWS)

    @pl.when(i == 0)
    def _():
        ext_ref[:, 0:HIST_ROWS, :] = jnp.zeros((1, HIST_ROWS, width), F32)

    @pl.when(i > 0)
    def _():
        ext_ref[:, 0:HIST_ROWS, :] = ext_ref[:, tp:tp + HIST_ROWS, :]

    ext_ref[:, HIST_ROWS:, :] = u_ref[...][None]
    pos = i * tp + lax.broadcasted_iota(I32, (1, tp, gw), 1)
    _pool_windows(ext_ref, pos, pw_ref, ps_ref, a_ref)


def _pool_prompt(u, pool_w, pool_scale):
    b, s, width = u.shape
    tp = TOKEN_TILE
    return pl.pallas_call(
        _pool_prompt_kernel,
        grid=(b, s // tp),
        in_specs=[pl.BlockSpec((None, tp, width), lambda bi, i: (bi, i, 0)),
                  pl.BlockSpec(pool_w.shape, lambda bi, i: (0, 0, 0)),
                  pl.BlockSpec(pool_scale.shape, lambda bi, i: (0, 0))],
        out_specs=pl.BlockSpec((None, tp, width), lambda bi, i: (bi, i, 0)),
        out_shape=jax.ShapeDtypeStruct((b, s, width), BF16),
        scratch_shapes=[pltpu.VMEM((1, HIST_ROWS + tp, width), F32)],
        compiler_params=_cparams(("arbitrary", "arbitrary")),
        name="pool_prompt",
    )(u, pool_w, pool_scale)


def _pool_sample_kernel(u_ref, hist_ref, pw_ref, ps_ref, a_ref, ext_ref, *, pos0):
    ks, t, width = u_ref.shape
    gw = width // len(POOL_WINDOWS)
    ext_ref[:, 0:HIST_ROWS, :] = hist_ref[...]
    ext_ref[:, HIST_ROWS:, :] = u_ref[...]
    pos = pos0 + lax.broadcasted_iota(I32, (1, t, gw), 1)
    _pool_windows(ext_ref, pos, pw_ref, ps_ref, a_ref)


def _pool_sample(u, hist16, pool_w, pool_scale, pos0):
    db, t, width = u.shape
    ks = TOKEN_TILE // t
    return pl.pallas_call(
        functools.partial(_pool_sample_kernel, pos0=pos0),
        grid=(db // ks,),
        in_specs=[pl.BlockSpec((ks, t, width), lambda i: (i, 0, 0)),
                  pl.BlockSpec((ks, HIST_ROWS, width), lambda i: (i, 0, 0)),
                  pl.BlockSpec(pool_w.shape, lambda i: (0, 0, 0)),
                  pl.BlockSpec(pool_scale.shape, lambda i: (0, 0))],
        out_specs=pl.BlockSpec((ks * t, width), lambda i: (i, 0)),
        out_shape=jax.ShapeDtypeStruct((db * t, width), BF16),
        scratch_shapes=[pltpu.VMEM((ks, HIST_ROWS + t, width), F32)],
        compiler_params=_cparams(("arbitrary",)),
        name="pool_sample",
    )(u, hist16, pool_w, pool_scale)


def _alibi_slopes(n_heads):
    return [2.0 ** (-8.0 * (h + 1) / n_heads) for h in range(n_heads)]


def _attn_prompt_kernel(qt_ref, qit_ref, wit_ref, kb_ref, vt_ref, kib_ref, o_ref,
                        sc_sc, scb_sc, qz_sc, acc_sc, lg_sc, al_sc, *, n_sel, idx_scale, n_heads, hd, idim, seq_len):
    qb = pl.program_id(1)
    tq = qt_ref.shape[-1]
    sk = tq
    nchunk = qb + 1
    t_idx = qb * tq + lax.broadcasted_iota(I32, (sk, tq), 1)
    s_loc = lax.broadcasted_iota(I32, (sk, tq), 0)

    def score_pair(c2, carry):
        for j in range(2):
            kic = kib_ref[2 * c2 + j]
            for h in range(IDX_HEADS):
                lg_sc[j, h] = _dot(kic, qit_ref[h * idim:(h + 1) * idim, :])
        for j in range(2):
            c = 2 * c2 + j
            acc = jnp.zeros((sk, tq), F32)
            for h in range(IDX_HEADS):
                acc = acc + wit_ref[h:h + 1, :] * jnp.maximum(lg_sc[j, h], 0.0)
            sc = jnp.where(c * sk + s_loc <= t_idx, acc * idx_scale, -jnp.inf)
            sc_sc[c] = sc
            scb_sc[c] = sc.astype(BF16).reshape(sk // BF16_ROWS, BF16_ROWS, tq)
        return carry

    lax.fori_loop(0, (nchunk + 1) // 2, score_pair, 0)

    def colsum(mask):
        return jnp.where(mask, 1.0, 0.0).reshape(sk // SUBLANES, SUBLANES, tq).sum(axis=0)

    def count(pred):
        def pair(c2, a):
            c = 2 * c2
            return a + colsum(pred(c, sc_sc[c])) + colsum(pred(c + 1, sc_sc[c + 1]))
        acc = lax.fori_loop(0, (nchunk + 1) // 2, pair, jnp.zeros((SUBLANES, tq), F32))
        return acc.sum(axis=0, keepdims=True)

    count_ge = lambda t: count(lambda c, s: s >= t)

    def count_ge_rounded(t):
        t_b = jnp.broadcast_to(t, (BF16_ROWS, tq)).astype(BF16)

        def pair(c2, a):
            for j in range(2):
                hit = jnp.where(scb_sc[2 * c2 + j] >= t_b, jnp.ones((), BF16), jnp.zeros((), BF16))
                rows = [hit[g] for g in range(sk // BF16_ROWS)]
                while len(rows) > 1:
                    rows = [x + y for x, y in zip(rows[0::2], rows[1::2])]
                a = a + rows[0].astype(F32)
            return a
        acc = lax.fori_loop(0, (nchunk + 1) // 2, pair, jnp.zeros((BF16_ROWS, tq), F32))
        return acc.sum(axis=0, keepdims=True)

    def wide_key(k16):
        return lax.shift_left(k16, 16) | jnp.where(k16 < 0, 0xFFFF, 0)

    def coarse(i, k16):
        trial = k16 + lax.shift_left(jnp.int32(1), 15 - i)
        k32 = wide_key(trial)
        ok = jnp.logical_or(count_ge_rounded(_key_to_float(k32)) >= n_sel, k32 < KEY_NEG_INF)
        return jnp.where(ok, trial, k16)

    k32 = wide_key(lax.fori_loop(0, 16, coarse, jnp.full((1, tq), -(2 ** 15), I32)))
    half_step, step = 2 ** 15, 2 ** 16

    def fine(i, bracket):
        lo, hi, n_lo = bracket
        mid = lo + ((hi - lo) >> 1)
        n_mid = count_ge(_key_to_float(mid))
        up = jnp.logical_and(hi - lo > 1, n_mid >= n_sel)
        down = jnp.logical_and(hi - lo > 1, n_mid < n_sel)
        return jnp.where(up, mid, lo), jnp.where(down, mid, hi), jnp.where(up, n_mid, n_lo)

    n_fine = int(math.ceil(math.log2(half_step + step)))
    tau_key, _, n_tau = lax.fori_loop(0, n_fine, fine, (jnp.maximum(k32 - half_step, KEY_NEG_INF), k32 + step,
                                                        jnp.full((1, tq), jnp.inf, F32)))
    tau = _key_to_float(tau_key)

    tie_rows = jnp.logical_and(n_tau > n_sel, tau_key > KEY_NEG_INF)
    has_tie = jnp.max(jnp.where(tie_rows, 1.0, 0.0)) > 0.5

    @pl.when(has_tie)
    def _():
        need = n_sel - count(lambda c, s: s > tau)
        nbits = max(1, int(math.ceil(math.log2(seq_len))))

        def first_enough(i, j):
            step = lax.shift_left(jnp.int32(1), nbits - 1 - i)
            upto = count(lambda c, s: jnp.logical_and(s == tau, c * sk + s_loc <= j + step - 1))
            return jnp.where(upto < need, j + step, j)

        last = lax.fori_loop(0, nbits, first_enough, jnp.zeros((1, tq), I32))

        def demote(c, carry):
            s = sc_sc[c]
            drop = jnp.logical_and(jnp.logical_and(tie_rows, s == tau), c * sk + s_loc > last)
            sc_sc[c] = jnp.where(drop, -jnp.inf, s)
            return carry

        lax.fori_loop(0, nchunk, demote, 0)

    pair_rows = lax.broadcasted_iota(I32, (2 * hd, tq), 0)
    for h in range(n_heads):
        blk = qt_ref[(h // 2) * 2 * hd:(h // 2 + 1) * 2 * hd, :]
        keep = pair_rows < hd if h % 2 == 0 else pair_rows >= hd
        qz_sc[h] = jnp.where(keep, blk, jnp.zeros_like(blk))
    acc_sc[...] = jnp.zeros(acc_sc.shape, F32)
    slopes = [s * LOG2E for s in _alibi_slopes(n_heads)]
    for h in range(n_heads):
        al_sc[h] = slopes[h] * s_loc.astype(F32)

    def att_chunk(c, j, m_all, l_all):
        sel = jnp.logical_and(sc_sc[c] >= tau, c * sk + s_loc <= t_idx)
        off = ((c - qb) * sk).astype(F32)
        m_out, l_out = [], []
        for h in range(n_heads):
            lg = jnp.where(sel, lg_sc[j, h] + al_sc[h], -jnp.inf)
            m_new = jnp.maximum(m_all[h], jnp.max(lg, axis=0, keepdims=True) + slopes[h] * off)
            alpha = jnp.exp2(m_all[h] - m_new)
            p = jnp.exp2(lg - (m_new - slopes[h] * off))
            l_out.append(alpha * l_all[h] + jnp.sum(p, axis=0, keepdims=True))
            m_out.append(m_new)
            pv = _dot(vt_ref[c, h * hd:(h + 1) * hd, :], p.astype(BF16))
            acc_sc[h * hd:(h + 1) * hd, :] = alpha * acc_sc[h * hd:(h + 1) * hd, :] + pv
        return tuple(m_out), tuple(l_out)

    def att_pair(c2, carry):
        for j in range(2):
            for h in range(n_heads):
                lg_sc[j, h] = _dot(kb_ref[2 * c2 + j, :, (h // 2) * 2 * hd:(h // 2 + 1) * 2 * hd], qz_sc[h])
        for j in range(2):
            carry = att_chunk(2 * c2 + j, j, *carry)
        return carry

    init = (tuple(jnp.full((1, tq), NEG_BIG, F32) for _ in range(n_heads)),
            tuple(jnp.zeros((1, tq), F32) for _ in range(n_heads)))
    _, l_all = lax.fori_loop(0, (nchunk + 1) // 2, att_pair, init)
    out_t = jnp.concatenate([acc_sc[h * hd:(h + 1) * hd, :] / l_all[h] for h in range(n_heads)], axis=0)
    o_ref[...] = out_t.T.astype(BF16)


def _attn_prompt(qt, qit, wit, kb, vt, kib, *, n_sel, idx_scale, n_heads, hd, idim):
    b, nck, aw, tq = qt.shape
    assert nck % 2 == 0 and IDX_HEADS <= n_heads
    iw = qit.shape[2]
    resident = lambda shape: pl.BlockSpec((None,) + shape, lambda bi, i: (bi, 0, 0, 0), pipeline_mode=pl.Buffered(1))
    tile = lambda rows: pl.BlockSpec((None, None, rows, tq), lambda bi, i: (bi, i, 0, 0))
    return pl.pallas_call(
        functools.partial(_attn_prompt_kernel, n_sel=n_sel, idx_scale=idx_scale, n_heads=n_heads, hd=hd, idim=idim,
                          seq_len=nck * tq),
        grid=(b, nck),
        in_specs=[tile(aw), tile(iw), tile(2 * SUBLANES),
                  resident((nck, tq, aw)), resident((nck, aw, tq)), resident((nck, tq, idim))],
        out_specs=pl.BlockSpec((None, tq, aw), lambda bi, i: (bi, i, 0)),
        out_shape=jax.ShapeDtypeStruct((b, nck * tq, aw), BF16),
        scratch_shapes=[pltpu.VMEM((nck, tq, tq), F32),
                        pltpu.VMEM((nck, tq // BF16_ROWS, BF16_ROWS, tq), BF16),
                        pltpu.VMEM((n_heads, 2 * hd, tq), BF16),
                        pltpu.VMEM((aw, tq), F32),
                        pltpu.VMEM((2, n_heads, tq, tq), F32),
                        pltpu.VMEM((n_heads, tq, tq), F32)],
        compiler_params=_cparams(("arbitrary", "arbitrary")),
        name="attn_prompt",
    )(qt, qit, wit, kb, vt, kib)


def _attn_sample_kernel(pt_ref, qbd_ref, qi_ref, wi_ref, knew_ref, vnew_ref, kinew_ref, *rest,
                        n_pages, page, n_sel, idx_scale, n_heads, hd, t_new):
    del pt_ref
    ki_refs, k_refs, v_refs = rest[0:n_pages], rest[n_pages:2 * n_pages], rest[2 * n_pages:3 * n_pages]
    o_ref, sc_sc, lg_sc = rest[3 * n_pages:]
    past = n_pages * page
    width = past + page
    group = n_heads // 2
    row = lax.broadcasted_iota(I32, (t_new, page), 0)
    lane = lax.broadcasted_iota(I32, (t_new, page), 1)
    wi = wi_ref[...]

    def scores(kit):
        rel = jnp.maximum(_dot(qi_ref[...], kit), 0.0)
        return jnp.sum(rel.reshape(IDX_HEADS, t_new, rel.shape[1]) * wi, axis=0) * idx_scale

    def page_pair(refs, pp, head_lo=None):
        if head_lo is None:
            parts = [refs[2 * pp + j][...] for j in range(2)]
        else:
            parts = [refs[2 * pp + j][head_lo:head_lo + group].reshape(group * hd, page) for j in range(2)]
        return jnp.concatenate(parts, axis=1).astype(BF16)

    for pp in range(n_pages // 2):
        sc_sc[:, 2 * pp * page:(2 * pp + 2) * page] = scores(page_pair(ki_refs, pp))
    sc_sc[:, past:width] = jnp.where(lane <= row, scores(kinew_ref[...]), -jnp.inf)

    def count(pred):
        return jnp.sum(jnp.where(pred(sc_sc[...]), 1.0, 0.0), axis=1, keepdims=True)

    count_ge = lambda t: count(lambda s: s >= t)
    tau_key = lax.fori_loop(0, 32 // SAMPLE_SEARCH_BITS,
                            lambda i, k: _threshold_step(i, k, count_ge, n_sel, SAMPLE_SEARCH_BITS),
                            jnp.full((t_new, 1), INT_MIN, I32))
    tau = _key_to_float(tau_key)

    tie_rows = jnp.logical_and(count_ge(tau) > n_sel, tau_key > KEY_NEG_INF)
    has_tie = jnp.max(jnp.where(tie_rows, 1.0, 0.0)) > 0.5
    pos = lax.broadcasted_iota(I32, (t_new, width), 1)

    @pl.when(has_tie)
    def _():
        need = n_sel - count(lambda s: s > tau)
        nbits = max(1, int(math.ceil(math.log2(width))))

        def first_enough(i, j):
            step = lax.shift_left(jnp.int32(1), nbits - 1 - i)
            upto = count(lambda s: jnp.logical_and(s == tau, pos <= j + step - 1))
            return jnp.where(upto < need, j + step, j)

        last = lax.fori_loop(0, nbits, first_enough, jnp.zeros((t_new, 1), I32))
        s = sc_sc[...]
        drop = jnp.logical_and(jnp.logical_and(tie_rows, s == tau), pos > last)
        sc_sc[...] = jnp.where(drop, -jnp.inf, s)

    slopes = [s * LOG2E for s in _alibi_slopes(n_heads)]
    t_pos = past + lax.broadcasted_iota(I32, (t_new, width), 0)
    sel = jnp.logical_and(sc_sc[...] >= tau, pos <= t_pos)
    bias = jnp.where(sel, 0.0, -jnp.inf)
    dist = (t_pos - pos).astype(F32)
    spans = [(2 * pp * page, 2 * page) for pp in range(n_pages // 2)] + [(past, page)]
    for g in range(2):
        for pp, (lo, n) in enumerate(spans):
            kt = page_pair(k_refs, pp, g * group) if pp < n_pages // 2 else knew_ref[g]
            lg = _dot(qbd_ref[g], kt).reshape(group, t_new, n)
            for hl in range(group):
                h = g * group + hl
                lg_sc[h, :, lo:lo + n] = lg[hl] - slopes[h] * dist[:, lo:lo + n] + bias[:, lo:lo + n]
    for g in range(2):
        probs, denoms = [], []
        for hl in range(group):
            lg = lg_sc[g * group + hl]
            pr = jnp.exp2(lg - jnp.max(lg, axis=1, keepdims=True))
            denoms.append(jnp.sum(pr, axis=1, keepdims=True))
            probs.append(pr)
        pg = jnp.concatenate(probs, axis=0).astype(BF16)
        acc = jnp.zeros((group * t_new, group * hd), F32)
        for pp, (lo, n) in enumerate(spans):
            vt = page_pair(v_refs, pp, g * group) if pp < n_pages // 2 else vnew_ref[g]
            acc = acc + _dot_nt(pg[:, lo:lo + n], vt)
        for hl in range(group):
            o_ref[g * group + hl] = acc[hl * t_new:(hl + 1) * t_new, hl * hd:(hl + 1) * hd] / denoms[hl]


def _attn_sample(page_table, qbd, qi2, wi3, knew, vnew, kinew, k_t, v_t, ki_t, layer, *, n_sel, idx_scale):
    db = qbd.shape[0]
    n_pages = page_table.shape[1]
    n_heads, hd, page = k_t.shape[2:]
    idim = ki_t.shape[2]
    t_new = wi3.shape[2]
    assert n_pages % 2 == 0 and n_heads % 2 == 0
    per_seq = lambda a: pl.BlockSpec((None,) + a.shape[1:], lambda b, pt: (b,) + (0,) * (a.ndim - 1))
    kv_page = lambda j: pl.BlockSpec((None, None, n_heads, hd, page), lambda b, pt: (layer, pt[b, j], 0, 0, 0))
    ki_page = lambda j: pl.BlockSpec((None, None, idim, page), lambda b, pt: (layer, pt[b, j], 0, 0))
    in_specs = ([per_seq(qbd), per_seq(qi2), per_seq(wi3), per_seq(knew), per_seq(vnew), per_seq(kinew)]
                + [ki_page(j) for j in range(n_pages)]
                + [kv_page(j) for j in range(n_pages)]
                + [kv_page(j) for j in range(n_pages)])
    grid_spec = pltpu.PrefetchScalarGridSpec(
        num_scalar_prefetch=1,
        grid=(db,),
        in_specs=in_specs,
        out_specs=pl.BlockSpec((None, n_heads, t_new, hd), lambda b, pt: (b, 0, 0, 0)),
        scratch_shapes=[pltpu.VMEM((t_new, (n_pages + 1) * page), F32),
                        pltpu.VMEM((n_heads, t_new, (n_pages + 1) * page), F32)],
    )
    return pl.pallas_call(
        functools.partial(_attn_sample_kernel, n_pages=n_pages, page=page, n_sel=n_sel, idx_scale=idx_scale,
                          n_heads=n_heads, hd=hd, t_new=t_new),
        grid_spec=grid_spec,
        out_shape=jax.ShapeDtypeStruct((db, n_heads, t_new, hd), F32),
        compiler_params=_cparams(("arbitrary",)),
        name="attn_sample",
    )(page_table, qbd, qi2, wi3, knew, vnew, kinew, *([ki_t] * n_pages), *([k_t] * n_pages), *([v_t] * n_pages))


def _layer_norm(y, g, b):
    mu = jnp.mean(y, axis=-1, keepdims=True)
    var = jnp.mean(jnp.square(y - mu), axis=-1, keepdims=True)
    return (y - mu) * lax.rsqrt(var + LN_EPS) * g + b


def _merge_kernel(x_ref, mod_ref, a_ref, o_ref, ga_ref, gb_ref, wa_ref, wb_ref, wo_ref, g_ref, b_ref, x1_ref, *, alpha):
    ks, r, d = x_ref.shape
    merged = (jax.nn.sigmoid(ga_ref[...]) * _dot(a_ref[...], wa_ref[...])
              + jax.nn.sigmoid(gb_ref[...]) * _dot(o_ref[...], wb_ref[...]))
    upd = _dot(merged.astype(BF16), wo_ref[...]).reshape(ks, r, d)
    y = alpha * x_ref[...] + mod_ref[2] * upd
    x1_ref[...] = _layer_norm(y, g_ref[...], b_ref[...])


def _merge(x3, mod4, ks, a, o, ga, gb, wa, wb, wo, ln_g, ln_b, alpha):
    g, r, d = x3.shape
    tm = ks * r
    row = lambda arr: pl.BlockSpec((tm, arr.shape[1]), lambda i: (i, 0))
    full = lambda arr: pl.BlockSpec(arr.shape, lambda i: (0,) * arr.ndim)
    return pl.pallas_call(
        functools.partial(_merge_kernel, alpha=alpha),
        grid=(g // ks,),
        in_specs=[pl.BlockSpec((ks, r, d), lambda i: (i, 0, 0)),
                  pl.BlockSpec((N_ADA, ks, 1, d), lambda i: (0, i, 0, 0)),
                  row(a), row(o), row(ga), row(gb), full(wa), full(wb), full(wo), full(ln_g), full(ln_b)],
        out_specs=pl.BlockSpec((ks, r, d), lambda i: (i, 0, 0)),
        out_shape=jax.ShapeDtypeStruct((g, r, d), F32),
        compiler_params=_cparams(("arbitrary",)),
        name="merge_ln1",
    )(x3, mod4, a, o, ga, gb, wa, wb, wo, ln_g, ln_b)


def _top_rows_exact(s, n_top):
    rows = s.shape[0]
    row_id = lax.broadcasted_iota(I32, s.shape, 0)
    rank = jnp.full(s.shape, n_top, I32)
    vals = []
    for r in range(n_top):
        m = jnp.max(s, axis=0, keepdims=True)
        idx = jnp.min(jnp.where(s == m, row_id, rows), axis=0, keepdims=True)
        hit = row_id == idx
        vals.append(m)
        rank = jnp.where(hit, r, rank)
        s = jnp.where(hit, -jnp.inf, s)
    return jnp.concatenate(vals, axis=0), rank


def _top_rows_distinct(s, n_top):
    rank = jnp.full(s.shape, n_top, I32)
    vals = []
    for r in range(n_top):
        m = jnp.max(s, axis=0, keepdims=True)
        hit = s == m
        vals.append(m)
        rank = jnp.where(hit, r, rank)
        s = jnp.where(hit, -jnp.inf, s)
    return jnp.concatenate(vals, axis=0), rank


def _rank_ties(rank, n_top):
    taken = jnp.sum(jnp.where(rank < n_top, 1.0, 0.0), axis=0, keepdims=True)
    return jnp.max(jnp.where(taken != n_top, 1.0, 0.0)) > 0.5


PEER_PAIRS = [(a, b) for a in range(PEER_TOPK) for b in range(PEER_TOPK) if (a + 1) * (b + 1) <= PEER_TOPK]
PEER_CAND_ROWS = -(-len(PEER_PAIRS) // SUBLANES) * SUBLANES


def _peer_kernel(x_ref, mod_ref, wq_ref, kt_ref, u_ref, vt_ref, g_ref, b_ref, y_ref,
                 h2_sc, st_sc, e0_sc, nb_sc, e1_sc, r1_sc, acc_sc, sv_sc, rk_sc, ts_sc, r2_sc, act_sc, w_sc,
                 *, alpha, n_keys):
    ks, r, d = x_ref.shape
    tm = ks * r
    step = pl.program_id(1)
    sub = PEER_EXPERT_TILE // n_keys

    @pl.when(step == 0)
    def _():
        h2 = (x_ref[...] * (1.0 + mod_ref[4]) + mod_ref[3]).reshape(tm, d).astype(BF16)
        h2_sc[...] = h2
        qp = _dot(h2, wq_ref[...]).astype(BF16)
        st_sc[...] = _dot_nt(kt_ref[...], qp)
        acc_sc[...] = jnp.zeros(acc_sc.shape, F32)

        def route(h, carry):
            s0 = st_sc[pl.ds(pl.multiple_of(h * 2 * n_keys, 2 * n_keys), n_keys), :]
            s1 = st_sc[pl.ds(pl.multiple_of(h * 2 * n_keys + n_keys, n_keys), n_keys), :]

            def halves(top):
                sv_sc[0], rk_sc[0] = top(s0, PEER_TOPK)
                sv_sc[1], rk_sc[1] = top(s1, PEER_TOPK)

            halves(_top_rows_distinct)

            @pl.when(jnp.logical_or(_rank_ties(rk_sc[0], PEER_TOPK), _rank_ties(rk_sc[1], PEER_TOPK)))
            def _():
                halves(_top_rows_exact)

            v0, v1, rank0 = sv_sc[0], sv_sc[1], rk_sc[0]
            pad = [jnp.full((PEER_CAND_ROWS - len(PEER_PAIRS), tm), -jnp.inf, F32)]
            cand = jnp.concatenate([v0[a:a + 1] + v1[b:b + 1] for a, b in PEER_PAIRS] + pad, axis=0)

            def pairs(top):
                ts_sc[...], r2_sc[...] = top(cand, PEER_TOPK)

            pairs(_top_rows_distinct)

            @pl.when(_rank_ties(r2_sc[...], PEER_TOPK))
            def _():
                pairs(_top_rows_exact)

            top_s, rank2 = ts_sc[...], r2_sc[...]
            z = jnp.sum(jnp.exp(top_s - top_s[0:1]), axis=0, keepdims=True)
            took = jnp.where(rank2 < PEER_TOPK, 1, 0)
            take = jnp.zeros(rank0.shape, I32)
            for a in range(PEER_TOPK):
                n_a = sum(took[i:i + 1] for i, (pa, _) in enumerate(PEER_PAIRS) if pa == a)
                take = jnp.where(rank0 == a, n_a, take)
            e0_sc[h] = (jnp.exp(s0 - v0[0:1]) / z * 0.5).reshape(n_keys // SUBLANES, SUBLANES, tm)
            nb_sc[h] = take.astype(F32).reshape(n_keys // SUBLANES, SUBLANES, tm)
            e1_sc[h] = jnp.exp(s1 - v1[0:1]).astype(BF16).reshape(n_keys // BF16_ROWS, BF16_ROWS, tm)
            r1_sc[h] = rk_sc[1].astype(F32).astype(BF16).reshape(n_keys // BF16_ROWS, BF16_ROWS, tm)
            return carry

        lax.fori_loop(0, PEER_HEADS, route, 0)

    n_tok = tm // PEER_TOKEN_SPLIT
    for t in range(PEER_TOKEN_SPLIT):
        act_sc[t] = _dot_nt(u_ref[...], h2_sc[t * n_tok:(t + 1) * n_tok, :])
    for t in range(PEER_TOKEN_SPLIT):
        lanes = slice(t * n_tok, (t + 1) * n_tok)
        for ii in range(sub):
            act = act_sc[t, ii * n_keys:(ii + 1) * n_keys, :]
            gate = jnp.zeros((n_keys // BF16_ROWS, BF16_ROWS, n_tok), BF16)
            for h in range(PEER_HEADS):
                take_i = jnp.broadcast_to(nb_sc[h, step, ii:ii + 1, lanes], (BF16_ROWS, n_tok)).astype(BF16)
                e0_i = jnp.broadcast_to(e0_sc[h, step, ii:ii + 1, lanes], (BF16_ROWS, n_tok)).astype(BF16)
                e1 = e1_sc[h, :, :, lanes]
                gate = gate + jnp.where(r1_sc[h, :, :, lanes] < take_i, e1, jnp.zeros_like(e1)) * e0_i
            gelu2 = act * (1.0 + lax.erf(act * (2.0 ** -0.5)))
            w = gelu2.astype(BF16).reshape(n_keys // BF16_ROWS, BF16_ROWS, n_tok) * gate
            w_sc[t, ii * n_keys:(ii + 1) * n_keys, :] = w.reshape(n_keys, n_tok)
    for t in range(PEER_TOKEN_SPLIT):
        acc_sc[:, t * n_tok:(t + 1) * n_tok] += _dot(vt_ref[...], w_sc[t])

    @pl.when(step == pl.num_programs(1) - 1)
    def _():
        f = acc_sc[...].T.reshape(ks, r, d)
        y = alpha * x_ref[...] + mod_ref[5] * f
        y_ref[...] = _layer_norm(y, g_ref[...], b_ref[...])


def _peer(x3, mod4, ks, wq, kt, u_b, vt_b, ln_g, ln_b, alpha, n_keys):
    g, r, d = x3.shape
    tm = ks * r
    n_exp = u_b.shape[0]
    assert PEER_EXPERT_TILE == SUBLANES * n_keys and n_exp % PEER_EXPERT_TILE == 0
    full = lambda arr: pl.BlockSpec(arr.shape, lambda i, e: (0,) * arr.ndim)
    return pl.pallas_call(
        functools.partial(_peer_kernel, alpha=alpha, n_keys=n_keys),
        grid=(g // ks, n_exp // PEER_EXPERT_TILE),
        in_specs=[pl.BlockSpec((ks, r, d), lambda i, e: (i, 0, 0)),
                  pl.BlockSpec((N_ADA, ks, 1, d), lambda i, e: (0, i, 0, 0)),
                  full(wq), full(kt),
                  pl.BlockSpec((PEER_EXPERT_TILE, d), lambda i, e: (e, 0)),
                  pl.BlockSpec((d, PEER_EXPERT_TILE), lambda i, e: (0, e)),
                  full(ln_g), full(ln_b)],
        out_specs=pl.BlockSpec((ks, r, d), lambda i, e: (i, 0, 0)),
        out_shape=jax.ShapeDtypeStruct((g, r, d), F32),
        scratch_shapes=[pltpu.VMEM((tm, d), BF16),
                        pltpu.VMEM((kt.shape[0], tm), F32),
                        pltpu.VMEM((PEER_HEADS, n_keys // SUBLANES, SUBLANES, tm), F32),
                        pltpu.VMEM((PEER_HEADS, n_keys // SUBLANES, SUBLANES, tm), F32),
                        pltpu.VMEM((PEER_HEADS, n_keys // BF16_ROWS, BF16_ROWS, tm), BF16),
                        pltpu.VMEM((PEER_HEADS, n_keys // BF16_ROWS, BF16_ROWS, tm), BF16),
                        pltpu.VMEM((d, tm), F32),
                        pltpu.VMEM((2, PEER_TOPK, tm), F32),
                        pltpu.VMEM((2, n_keys, tm), I32),
                        pltpu.VMEM((PEER_TOPK, tm), F32),
                        pltpu.VMEM((PEER_CAND_ROWS, tm), I32),
                        pltpu.VMEM((PEER_TOKEN_SPLIT, PEER_EXPERT_TILE, tm // PEER_TOKEN_SPLIT), F32),
                        pltpu.VMEM((PEER_TOKEN_SPLIT, PEER_EXPERT_TILE, tm // PEER_TOKEN_SPLIT), BF16)],
        compiler_params=_cparams(("arbitrary", "arbitrary")),
        name="peer_ln2",
    )(x3, mod4, wq, kt, u_b, vt_b, ln_g, ln_b)


def _layer_weights(w_in, pool_w, w_branch_a, w_branch_b, w_out, peer_wq, peer_sub_keys, peer_u, peer_v,
                   *, pw, aw, iw, idim):
    d = w_in.shape[0]
    o = np.cumsum([0, pw, aw, aw, aw, iw, idim, IDX_HEADS, d, d])
    cut = lambda j: w_in[:, o[j]:o[j + 1]]
    u_w, q_w, k_w, v_w, qi_w, ki_w, wi_w, ga_w, gb_w = (cut(j) for j in range(9))
    wn = jnp.concatenate([u_w, k_w], axis=1).astype(BF16)
    wg = jnp.concatenate([ga_w, gb_w], axis=1).astype(BF16)
    wi_pad = jnp.pad(wi_w, ((0, 0), (0, 2 * SUBLANES - IDX_HEADS)))
    wt = jnp.concatenate([v_w, q_w, qi_w, wi_pad, k_w, ki_w], axis=1).T.astype(BF16)
    n_keys, half = peer_sub_keys.shape[1], peer_sub_keys.shape[2]
    kt = jnp.einsum("hg,cb,ckd->hckgbd", jnp.eye(PEER_HEADS, dtype=F32), jnp.eye(2, dtype=F32), peer_sub_keys)
    kt = kt.reshape(PEER_HEADS * 2 * n_keys, PEER_HEADS * 2 * half).astype(BF16)
    return dict(wn=wn, wki=ki_w.astype(BF16), wg=wg, wt=wt, pool_w=pool_w.astype(BF16),
                wa=w_branch_a.astype(BF16), wb=w_branch_b.astype(BF16), wo=w_out.astype(BF16),
                wq=peer_wq.astype(BF16), kt=kt, u_b=peer_u.astype(BF16), vt_b=peer_v.T.astype(BF16))


def kernel(x_prompt, x_sample, cache_k, cache_v, cache_kidx, state_pool, page_table, c_prompt, c_sample,
           w_ada, b_ada, w_in, pool_w, pool_scale, w_branch_a, w_branch_b, w_out, ln1_g, ln1_b,
           peer_wq, peer_sub_keys, peer_u, peer_v, ln2_g, ln2_b):
    depth = w_ada.shape[0]
    b, s, d = x_prompt.shape
    db, t_new, _ = x_sample.shape
    page, n_heads, hd = cache_k.shape[2:]
    idim = cache_kidx.shape[3]
    pool_hist, pw = state_pool.shape[2:]
    n_pages = page_table.shape[1]
    past = n_pages * page
    aw, iw = n_heads * hd, IDX_HEADS * idim
    n_keys = peer_sub_keys.shape[2]
    alpha = (2 * depth) ** 0.25
    idx_scale = float(iw) ** -0.5
    q_scale = float(hd) ** -0.5 * LOG2E
    assert t_new == SUBLANES and n_heads == SUBLANES and pool_hist < HIST_ROWS and s % TOKEN_TILE == 0

    n_c = b + db
    c_all = jnp.pad(jnp.concatenate([c_prompt, c_sample], axis=0), ((0, -n_c % SUBLANES), (0, 0)))
    y_p, y_s = x_prompt, x_sample
    k_t = jnp.transpose(cache_k, (0, 1, 3, 4, 2))
    v_t = jnp.transpose(cache_v, (0, 1, 3, 4, 2))
    ki_t = jnp.transpose(cache_kidx, (0, 1, 3, 2))
    outs = [[] for _ in range(8)]
    for l in range(depth):
        w = _layer_weights(w_in[l], pool_w[l], w_branch_a[l], w_branch_b[l], w_out[l], peer_wq[l], peer_sub_keys[l],
                           peer_u[l], peer_v[l], pw=pw, aw=aw, iw=iw, idim=idim)
        mod = _ada(c_all, w_ada[l], b_ada[l])
        mod_p, mod_s = mod[:, :b], mod[:, b:n_c]
        ps, g1, b1, g2, b2 = pool_scale[l][None], ln1_g[l][None], ln1_b[l][None], ln2_g[l][None], ln2_b[l][None]
        proj = functools.partial(_proj, wn=w["wn"], wki=w["wki"], wg=w["wg"], wt=w["wt"],
                                 pw=pw, aw=aw, iw=iw, idim=idim, q_scale=q_scale)

        x3, mod4, ks = _grouped(y_p, mod_p, TOKEN_TILE)
        nck = s // TOKEN_TILE
        u, kb, kib, kt32, vt32, kit32, vt, qt, qit, wit, ga, gb = proj(x3, mod4, ks, seq_chunks=nck)
        a = _pool_prompt(u.reshape(b, s, pw), w["pool_w"], ps)
        o = _attn_prompt(qt.reshape(b, nck, aw, TOKEN_TILE), qit.reshape(b, nck, iw, TOKEN_TILE),
                         wit.reshape(b, nck, 2 * SUBLANES, TOKEN_TILE), kb.reshape(b, nck, TOKEN_TILE, aw),
                         vt.reshape(b, nck, aw, TOKEN_TILE), kib.reshape(b, nck, TOKEN_TILE, idim),
                         n_sel=min(TOPK_MAX, s // 4), idx_scale=idx_scale, n_heads=n_heads, hd=hd, idim=idim)
        x1 = _merge(x3, mod4, ks, a.reshape(b * s, pw), o.reshape(b * s, aw), ga, gb,
                    w["wa"], w["wb"], w["wo"], g1, b1, alpha)
        x3p, mod4p, ksp = _grouped(x1.reshape(b, s, d), mod_p, PEER_TOKEN_TILE)
        y_p = _peer(x3p, mod4p, ksp, w["wq"], w["kt"], w["u_b"], w["vt_b"], g2, b2, alpha, n_keys).reshape(b, s, d)
        heads_last = lambda z: jnp.transpose(z.reshape(z.shape[0], n_heads, hd, z.shape[2]), (0, 3, 1, 2))
        outs[0].append(heads_last(kt32))
        outs[1].append(heads_last(vt32))
        outs[2].append(jnp.transpose(kit32, (0, 2, 1)))
        outs[3].append(u.reshape(b, s, pw)[:, s - pool_hist:])

        x3, mod4, ks = _grouped(y_s, mod_s, TOKEN_TILE)
        u, kb, kib, kt32, vt32, kit32, vt, qt, qit, wit, ga, gb = proj(x3, mod4, ks)
        hist = state_pool[l]
        hist16 = jnp.pad(hist, ((0, 0), (HIST_ROWS - pool_hist, 0), (0, 0)))
        a = _pool_sample(u.reshape(db, t_new, pw), hist16, w["pool_w"], ps, past)
        untile = lambda z: jnp.moveaxis(z, 0, 1).reshape(z.shape[1], db, t_new)
        group = n_heads // 2
        q_s = jnp.transpose(untile(qt).reshape(2, group, hd, db, t_new), (3, 0, 1, 4, 2))
        qbd = jnp.einsum("bgltd,lm->bgltmd", q_s, jnp.eye(group, dtype=BF16)).reshape(db, 2, group * t_new, group * hd)
        qi2 = jnp.transpose(untile(qit).reshape(IDX_HEADS, idim, db, t_new), (2, 0, 3, 1)).reshape(db, IDX_HEADS * t_new, idim)
        wi3 = jnp.transpose(untile(wit)[:IDX_HEADS], (1, 0, 2))[..., None]
        lane_pad = lambda z: jnp.pad(z, ((0, 0),) * (z.ndim - 1) + ((0, page - t_new),))
        knew = lane_pad(jnp.transpose(kb.reshape(db, t_new, 2, group * hd), (0, 2, 3, 1)))
        vnew = lane_pad(jnp.transpose(untile(vt).reshape(2, group * hd, db, t_new), (2, 0, 1, 3)))
        kinew = lane_pad(jnp.transpose(kib.reshape(db, t_new, idim), (0, 2, 1)))
        o = _attn_sample(page_table, qbd, qi2, wi3, knew, vnew, kinew, k_t, v_t, ki_t, l,
                         n_sel=min(TOPK_MAX, (past + t_new) // 4), idx_scale=idx_scale)
        o = jnp.transpose(o, (0, 2, 1, 3)).reshape(db * t_new, aw).astype(BF16)
        x1 = _merge(x3, mod4, ks, a, o, ga, gb, w["wa"], w["wb"], w["wo"], g1, b1, alpha)
        x3p, mod4p, ksp = _grouped(x1, mod_s, PEER_TOKEN_TILE)
        y_s = _peer(x3p, mod4p, ksp, w["wq"], w["kt"], w["u_b"], w["vt_b"], g2, b2, alpha, n_keys)
        tokens_first = lambda z: jnp.transpose(untile(z), (1, 2, 0))
        outs[4].append(tokens_first(kt32).reshape(db, t_new, n_heads, hd))
        outs[5].append(tokens_first(vt32).reshape(db, t_new, n_heads, hd))
        outs[6].append(tokens_first(kit32))
        outs[7].append(jnp.concatenate([hist, u.reshape(db, t_new, pw)], axis=1)[:, t_new:])
    return (y_p, y_s) + tuple(jnp.stack(o_) for o_ in outs)
```
